```python
import math
import jax, jax.numpy as jnp
from jax import lax
import numpy as np

D_MODEL = 1024
BATCH = 8
SEQ = 4096
DEPTH = 1

CHUNK = 64
Q_BLOCK = 128
ROPE_THETA = 10000.0
EPS = 1e-6
DA_HEADS = 8
DA_V_DIM = D_MODEL // DA_HEADS
DA_QK_DIM = DA_V_DIM // 2
DA_WIDTH = DA_HEADS * DA_V_DIM
RET_HEADS = 4
RET_V_DIM = D_MODEL // RET_HEADS
RET_K_DIM = RET_V_DIM // 2
RET_WIDTH = RET_HEADS * RET_V_DIM
D_FF = 4 * D_MODEL
N_BRANCH = 2
N_MOD = 6
COL_SIZES = (
    DA_HEADS * 2 * DA_QK_DIM,
    DA_HEADS * 2 * DA_QK_DIM,
    DA_WIDTH,
    RET_HEADS * RET_K_DIM,
    RET_HEADS * RET_K_DIM,
    RET_WIDTH,
    RET_WIDTH,
    N_BRANCH * D_MODEL,
)
COL_SPLITS = tuple(int(s) for s in np.cumsum(COL_SIZES)[:-1])
IN_COLS = int(sum(COL_SIZES))

kernel_name = "hybrid_diffattn_retention_block"


def rms_norm(x, g):
    x32 = x.astype(jnp.float32)
    y = x32 * lax.rsqrt(jnp.mean(x32 * x32, axis=-1, keepdims=True) + EPS)
    return (y * g.astype(jnp.float32)).astype(x.dtype)


def rope(x, inv_freq):
    s = x.shape[-2]
    ang = jnp.arange(s, dtype=jnp.float32)[:, None] * inv_freq[None, :]
    cos = jnp.cos(ang).astype(x.dtype)
    sin = jnp.sin(ang).astype(x.dtype)
    x1, x2 = jnp.split(x, 2, axis=-1)
    return jnp.concatenate([x1 * cos - x2 * sin, x2 * cos + x1 * sin], axis=-1)


def diff_attention(q, k, v, lam):
    b, h, _, s, d = q.shape
    nb = s // Q_BLOCK
    qb = q.reshape(b, h, 2, nb, Q_BLOCK, d).transpose(3, 0, 1, 2, 4, 5)
    key_chunk = jnp.arange(s) // CHUNK

    def block(args):
        qi, bi = args
        sc = jnp.einsum('bhmqd,bhmkd->bhmqk', qi, k).astype(jnp.float32)
        q_chunk = (bi * Q_BLOCK + jnp.arange(Q_BLOCK)) // CHUNK
        mask = key_chunk[None, :] <= q_chunk[:, None]
        p = jax.nn.softmax(jnp.where(mask, sc, -jnp.inf), axis=-1)
        a = p[:, :, 0] - lam * p[:, :, 1]
        return jnp.einsum('bhqk,bhkd->bhqd', a.astype(v.dtype), v)

    out = lax.map(block, (qb, jnp.arange(nb)))
    return out.transpose(1, 2, 0, 3, 4).reshape(b, h, s, v.shape[-1])


def retention(q, k, v, log_gamma):
    b, h, s, dk = q.shape
    dv = v.shape[-1]
    nc = s // CHUNK
    f32 = jnp.float32

    def to_chunks(t):
        return t.astype(f32).reshape(b, h, nc, CHUNK, t.shape[-1]).transpose(2, 0, 1, 3, 4)

    n = jnp.arange(CHUNK, dtype=f32)
    lg = log_gamma[:, None, None]
    diff = n[:, None] - n[None, :]
    d_intra = jnp.where(diff >= 0, jnp.exp(jnp.maximum(diff, 0.0) * lg), 0.0)
    inner_decay = jnp.exp((n + 1.0)[None, :] * log_gamma[:, None])[..., None]
    kv_decay = jnp.exp((CHUNK - 1.0 - n)[None, :] * log_gamma[:, None])[..., None]
    chunk_decay = jnp.exp(CHUNK * log_gamma)[:, None, None]

    def step(state, inp):
        qc, kc, vc = inp
        sc = jnp.einsum('bhid,bhjd->bhij', qc, kc) * d_intra
        y = (jnp.einsum('bhij,bhjv->bhiv', sc, vc)
             + jnp.einsum('bhid,bhdv->bhiv', qc, state) * inner_decay)
        state = state * chunk_decay + jnp.einsum('bhjd,bhjv->bhdv', kc * kv_decay, vc)
        return state, y

    state0 = jnp.zeros((b, h, dk, dv), f32)
    _, ys = lax.scan(step, state0, (to_chunks(q), to_chunks(k), to_chunks(v)))
    return ys.transpose(1, 2, 0, 3, 4).reshape(b, h, s, dv).astype(v.dtype)


def setup_inputs(seed: int = 0) -> dict:
    key = jax.random.key(seed)
    ks = jax.random.split(key, 20)
    f32 = jnp.float32
    nrm = lambda k, shp, sc: (jax.random.normal(k, shp, f32) * sc).astype(f32)
    gain = lambda k, shp: 1.0 + 0.02 * jax.random.normal(k, shp, f32)
    return {
        "x": nrm(ks[0], (BATCH, SEQ, D_MODEL), 1.0),
        "c": nrm(ks[1], (BATCH, D_MODEL), 1.0),
        "w_ada": nrm(ks[2], (DEPTH, D_MODEL, N_MOD * D_MODEL), 0.2 * D_MODEL ** -0.5),
        "b_ada": nrm(ks[3], (DEPTH, N_MOD * D_MODEL), 0.01),
        "g_norm1": gain(ks[4], (DEPTH, D_MODEL)),
        "w_in": nrm(ks[5], (DEPTH, D_MODEL, IN_COLS), D_MODEL ** -0.5),
        "g_q": gain(ks[6], (DEPTH, DA_QK_DIM)),
        "g_k": gain(ks[7], (DEPTH, DA_QK_DIM)),
        "lambda_q1": nrm(ks[8], (DEPTH, DA_QK_DIM), 0.1),
        "lambda_k1": nrm(ks[9], (DEPTH, DA_QK_DIM), 0.1),
        "lambda_q2": nrm(ks[10], (DEPTH, DA_QK_DIM), 0.1),
        "lambda_k2": nrm(ks[11], (DEPTH, DA_QK_DIM), 0.1),
        "g_da_out": gain(ks[12], (DEPTH, DA_V_DIM)),
        "g_ret_out": gain(ks[13], (DEPTH, RET_V_DIM)),
        "w_out": nrm(ks[14], (DEPTH, D_MODEL, D_MODEL), D_MODEL ** -0.5),
        "g_norm2": gain(ks[15], (DEPTH, D_MODEL)),
        "w_up": nrm(ks[16], (DEPTH, D_MODEL, D_FF), D_MODEL ** -0.5),
        "w_down": nrm(ks[17], (DEPTH, D_FF, D_MODEL), D_FF ** -0.5),
    }


def reference(x, c, w_ada, b_ada, g_norm1, w_in, g_q, g_k, lambda_q1, lambda_k1,
              lambda_q2, lambda_k2, g_da_out, g_ret_out, w_out, g_norm2, w_up, w_down):
    b, s, d = x.shape
    f32 = jnp.float32
    da_inv_freq = ROPE_THETA ** (-jnp.arange(0, DA_QK_DIM, 2, dtype=f32) / DA_QK_DIM)
    ret_inv_freq = 1.0 / (ROPE_THETA ** jnp.linspace(0.0, 1.0, RET_K_DIM // 2, dtype=f32))
    log_gamma = jnp.asarray(np.log(1.0 - 2.0 ** (-5.0 - np.arange(RET_HEADS))).astype(np.float32))

    for l in range(DEPTH):
        lambda_init = 0.8 - 0.6 * math.exp(-0.3 * l)
        mod = jax.nn.silu(c) @ w_ada[l] + b_ada[l]
        shift1, scale1, gate1, shift2, scale2, gate2 = [m[:, None, :] for m in jnp.split(mod, N_MOD, axis=-1)]

        hmix = rms_norm(x, g_norm1[l]) * (1.0 + scale1) + shift1
        proj = hmix @ w_in[l]
        qa, ka, va, qr, kr, vr, gr_swish, merge = jnp.split(proj, COL_SPLITS, axis=-1)

        qa = qa.reshape(b, s, DA_HEADS, 2, DA_QK_DIM).transpose(0, 2, 3, 1, 4)
        ka = ka.reshape(b, s, DA_HEADS, 2, DA_QK_DIM).transpose(0, 2, 3, 1, 4)
        va = va.reshape(b, s, DA_HEADS, DA_V_DIM).transpose(0, 2, 1, 3)
        qa = rope(rms_norm(qa, g_q[l]), da_inv_freq) * (DA_QK_DIM ** -0.5)
        ka = rope(rms_norm(ka, g_k[l]), da_inv_freq)
        lam = (jnp.exp(jnp.sum(lambda_q1[l].astype(f32) * lambda_k1[l].astype(f32)))
               - jnp.exp(jnp.sum(lambda_q2[l].astype(f32) * lambda_k2[l].astype(f32)))
               + lambda_init)
        oa = diff_attention(qa, ka, va, lam)
        oa = rms_norm(oa, g_da_out[l]) * (1.0 - lambda_init)
        oa = oa.transpose(0, 2, 1, 3).reshape(b, s, DA_WIDTH)

        qr = qr.reshape(b, s, RET_HEADS, RET_K_DIM).transpose(0, 2, 1, 3)
        kr = kr.reshape(b, s, RET_HEADS, RET_K_DIM).transpose(0, 2, 1, 3)
        vr = vr.reshape(b, s, RET_HEADS, RET_V_DIM).transpose(0, 2, 1, 3)
        qr = rope(qr, ret_inv_freq)
        kr = rope(kr, ret_inv_freq) * (RET_K_DIM ** -0.5)
        orr = rms_norm(retention(qr, kr, vr, log_gamma), g_ret_out[l])
        orr = orr.transpose(0, 2, 1, 3).reshape(b, s, RET_WIDTH) * jax.nn.silu(gr_swish)

        ga, gb = jnp.split(jax.nn.sigmoid(merge), N_BRANCH, axis=-1)
        y = (ga * oa + gb * orr) @ w_out[l]
        x = x + gate1 * y

        hff = rms_norm(x, g_norm2[l]) * (1.0 + scale2) + shift2
        ff = jnp.square(jax.nn.relu(hff @ w_up[l])) @ w_down[l]
        x = x + gate2 * ff
    return x
```

```python
import functools
import math

import numpy as np
import jax
import jax.numpy as jnp
from jax import lax
from jax.experimental import pallas as pl
from jax.experimental.pallas import tpu as pltpu

D_MODEL = 1024
CHUNK = 64
ROPE_THETA = 10000.0
EPS = 1e-6
DA_HEADS = 8
DA_V_DIM = D_MODEL // DA_HEADS
DA_QK_DIM = DA_V_DIM // 2
RET_HEADS = 4
RET_V_DIM = D_MODEL // RET_HEADS
RET_K_DIM = RET_V_DIM // 2
D_FF = 4 * D_MODEL
N_MOD = 6
IN_COLS = 8 * D_MODEL

C_QA, C_KA, C_VA, C_QR, C_KR, C_VR, C_GR, C_GA, C_GB = (
    0, 1024, 2048, 3072, 3584, 4096, 5120, 6144, 7168)

LANES = 128
V7X_VMEM_LIMIT = 56 * 1024 * 1024

TM_PROJ = 256
TQ = 256
TK = 256
RET_CHUNK = 256
TM_MLP = 256

F32 = jnp.float32
BF16 = jnp.bfloat16
NT_DIMS = (((1,), (1,)), ((), ()))
TN_DIMS = (((0,), (0,)), ((), ()))


def _resident(shape):
    return pl.BlockSpec(shape, lambda *_: (0,) * len(shape),
                        pipeline_mode=pl.Buffered(1))


def _mod_kernel(c_ref, w_ref, b_ref, lq1_ref, lk1_ref, lq2_ref, lk2_ref,
                mod_ref, lam_ref, *, lambda_init):
    c = c_ref[...]
    sc = (c * jax.nn.sigmoid(c)).astype(BF16)
    mod_ref[...] = jnp.dot(sc, w_ref[...].astype(BF16),
                           preferred_element_type=F32) + b_ref[...]
    s1 = jnp.sum(lq1_ref[...] * lk1_ref[...], axis=-1, keepdims=True)
    s2 = jnp.sum(lq2_ref[...] * lk2_ref[...], axis=-1, keepdims=True)
    lam_ref[...] = jnp.exp(s1) - jnp.exp(s2) + lambda_init


def _modulation(c, w_ada, b_ada, lq1, lk1, lq2, lk2, lambda_init):
    b, d = c.shape
    n = w_ada.shape[1]
    bn = 1536
    vec = pl.BlockSpec((1, DA_QK_DIM), lambda j: (0, 0))
    return pl.pallas_call(
        functools.partial(_mod_kernel, lambda_init=lambda_init),
        grid=(n // bn,),
        in_specs=[pl.BlockSpec((b, d), lambda j: (0, 0)),
                  pl.BlockSpec((d, bn), lambda j: (0, j)),
                  pl.BlockSpec((1, bn), lambda j: (0, j)),
                  vec, vec, vec, vec],
        out_specs=[pl.BlockSpec((b, bn), lambda j: (0, j)),
                   pl.BlockSpec((1, 1), lambda j: (0, 0))],
        out_shape=[jax.ShapeDtypeStruct((b, n), F32),
                   jax.ShapeDtypeStruct((1, 1), F32)],
        compiler_params=pltpu.CompilerParams(
            dimension_semantics=("arbitrary",),
            vmem_limit_bytes=V7X_VMEM_LIMIT),
        name="modulation",
    )(c, w_ada, b_ada, lq1, lk1, lq2, lk2)


def _in_proj_kernel(x_ref, mod_ref, g1_ref, w_ref, wvt_ref, gq_ref, gk_ref,
                    cda_ref, sda_a_ref, sda_b_ref, cr_ref, sr_ref,
                    qa_ref, ka_ref, vt_ref, qkr_ref, vr_ref, ga_ref, gbs_ref):
    tm = x_ref.shape[0]
    x = x_ref[...]
    ms = jnp.mean(x * x, axis=-1, keepdims=True)
    h = x * lax.rsqrt(ms + EPS) * g1_ref[...]
    mod = mod_ref[0]
    hm = (h * (1.0 + mod[1:2, :]) + mod[0:1, :]).astype(BF16)

    def proj(col, width):
        return jnp.dot(hm, w_ref[:, col:col + width], preferred_element_type=F32)

    gi = lax.broadcasted_iota(jnp.int32, (256, 256), 0) // DA_QK_DIM
    gj = lax.broadcasted_iota(jnp.int32, (256, 256), 1) // DA_QK_DIM
    group_sum = jnp.where(gi == gj, 1.0, 0.0).astype(BF16)

    cda, sda_a, sda_b = cda_ref[...], sda_a_ref[...], sda_b_ref[...]

    def qk_norm_rope(acc, gain, out_ref):
        sq = (acc * acc).astype(BF16)
        for blk in range(D_MODEL // 256):
            ssq = jnp.dot(sq[:, blk * 256:(blk + 1) * 256], group_sum,
                          preferred_element_type=F32)
            rs = lax.rsqrt(ssq * (1.0 / DA_QK_DIM) + EPS)
            for half in range(2):
                c0 = blk * 256 + half * LANES
                a = acc[:, c0:c0 + LANES] * rs[:, half * LANES:(half + 1) * LANES] * gain
                rot = (a * cda + pltpu.roll(a, LANES - 32, 1) * sda_a
                       + pltpu.roll(a, 32, 1) * sda_b)
                out_ref[:, c0:c0 + LANES] = rot.astype(BF16)

    qk_norm_rope(proj(C_QA, D_MODEL), gq_ref[...], qa_ref)
    qk_norm_rope(proj(C_KA, D_MODEL), gk_ref[...], ka_ref)

    vt = lax.dot_general(wvt_ref[...], hm, NT_DIMS, preferred_element_type=F32)
    for cblk in range(tm // TK):
        vt_ref[0, cblk] = vt[:, cblk * TK:(cblk + 1) * TK].astype(BF16)

    qkr = proj(C_QR, D_MODEL)
    cr, sr = cr_ref[...], sr_ref[...]
    for j in range(D_MODEL // LANES):
        a = qkr[:, j * LANES:(j + 1) * LANES]
        rot = a * cr + pltpu.roll(a, LANES // 2, 1) * sr
        if j >= RET_HEADS:
            rot = rot * (RET_K_DIM ** -0.5)
        qkr_ref[:, j * LANES:(j + 1) * LANES] = rot.astype(BF16)

    vr_ref[...] = proj(C_VR, D_MODEL).astype(BF16)

    gr = proj(C_GR, D_MODEL)
    ga_ref[...] = jax.nn.sigmoid(proj(C_GA, D_MODEL)).astype(BF16)
    gbs_ref[...] = (jax.nn.sigmoid(proj(C_GB, D_MODEL))
                    * (gr * jax.nn.sigmoid(gr))).astype(BF16)


def _in_proj(x2, mod3, g1, w_in, w_vt, gq, gk, tabs, batch, seq):
    t, d = x2.shape
    tm = TM_PROJ
    tpb = seq // tm
    row = pl.BlockSpec((tm, d), lambda i: (i, 0))
    tab = pl.BlockSpec((tm, LANES), lambda i: (i % tpb, 0))
    vec = pl.BlockSpec((1, LANES), lambda i: (0, 0))
    out_row = jax.ShapeDtypeStruct((t, d), BF16)
    return pl.pallas_call(
        _in_proj_kernel,
        grid=(t // tm,),
        in_specs=[row,
                  pl.BlockSpec((1, N_MOD, d), lambda i: (i // tpb, 0, 0)),
                  pl.BlockSpec((1, d), lambda i: (0, 0)),
                  _resident((d, IN_COLS)),
                  _resident((d, d)),
                  vec, vec, tab, tab, tab, tab, tab],
        out_specs=[row, row,
                   pl.BlockSpec((1, tm // TK, d, TK),
                                lambda i: (i // tpb, i % tpb, 0, 0)),
                   row, row, row, row],
        out_shape=[out_row, out_row,
                   jax.ShapeDtypeStruct((batch, seq // TK, d, TK), BF16),
                   out_row, out_row, out_row, out_row],
        compiler_params=pltpu.CompilerParams(
            dimension_semantics=("arbitrary",),
            vmem_limit_bytes=V7X_VMEM_LIMIT),
        name="in_proj",
    )(x2, mod3, g1, w_in, w_vt, gq, gk, *tabs)


def _diff_attn_kernel(q_ref, k_ref, vt_ref, lam_ref, g_ref, o_ref,
                      acc_ref, m_ref, l_ref):
    qi = pl.program_id(2)
    q = q_ref[0]
    lane = lax.broadcasted_iota(jnp.int32, q.shape, 1)
    zero = jnp.zeros_like(q)
    qcat = jnp.concatenate([jnp.where(lane < DA_QK_DIM, q, zero),
                            jnp.where(lane >= DA_QK_DIM, q, zero)], axis=0)

    m_ref[...] = jnp.full(m_ref.shape, -jnp.inf, F32)
    l_ref[...] = jnp.zeros(l_ref.shape, F32)
    acc_ref[...] = jnp.zeros(acc_ref.shape, F32)

    def step(j, masked):
        koff = pl.multiple_of(j * TK, TK)
        kblk = k_ref[0, pl.ds(koff, TK), :]
        s = lax.dot_general(kblk, qcat, NT_DIMS, preferred_element_type=F32)
        if masked:
            kc = lax.broadcasted_iota(jnp.int32, s.shape, 0) // CHUNK
            qc = (lax.broadcasted_iota(jnp.int32, s.shape, 1) % TQ) // CHUNK
            s = jnp.where(kc <= qc, s, -jnp.inf)
        m_old = m_ref[...]
        m_new = jnp.maximum(m_old, jnp.max(s, axis=0, keepdims=True))
        alpha = jnp.exp(m_old - m_new)
        e = jnp.exp(s - m_new)
        l_ref[...] = alpha * l_ref[...] + jnp.sum(e, axis=0, keepdims=True)
        pv = jnp.dot(vt_ref[0, j], e.astype(BF16), preferred_element_type=F32)
        acc_ref[...] = alpha * acc_ref[...] + pv
        m_ref[...] = m_new

    def body(j, carry):
        step(j, False)
        return carry

    lax.fori_loop(0, qi, body, 0)
    step(qi, True)

    acc = acc_ref[...]
    l = l_ref[...]
    o = acc[:, :TQ] / l[:, :TQ] - lam_ref[...] * (acc[:, TQ:] / l[:, TQ:])
    ms = jnp.mean(o * o, axis=0, keepdims=True)
    on = o * lax.rsqrt(ms + EPS) * g_ref[...]
    o_ref[0] = on.T.astype(BF16)


def _diff_attn(qa, ka, vt, lam, g_col, batch, seq):
    nq = seq // TQ
    return pl.pallas_call(
        _diff_attn_kernel,
        grid=(batch, DA_HEADS, nq),
        in_specs=[pl.BlockSpec((1, TQ, LANES), lambda b, h, i: (b, i, h)),
                  pl.BlockSpec((1, seq, LANES), lambda b, h, i: (b, 0, h)),
                  pl.BlockSpec((1, seq // TK, DA_V_DIM, TK), lambda b, h, i: (b, 0, h, 0)),
                  pl.BlockSpec((1, 1), lambda b, h, i: (0, 0)),
                  pl.BlockSpec((DA_V_DIM, 1), lambda b, h, i: (0, 0))],
        out_specs=pl.BlockSpec((1, TQ, LANES), lambda b, h, i: (b, i, h)),
        out_shape=jax.ShapeDtypeStruct((batch, seq, D_MODEL), BF16),
        scratch_shapes=[pltpu.VMEM((DA_V_DIM, 2 * TQ), F32),
                        pltpu.VMEM((1, 2 * TQ), F32),
                        pltpu.VMEM((1, 2 * TQ), F32)],
        compiler_params=pltpu.CompilerParams(
            dimension_semantics=("arbitrary", "arbitrary", "arbitrary"),
            vmem_limit_bytes=V7X_VMEM_LIMIT),
        name="diff_attn",
    )(qa, ka, vt, lam, g_col)


def _retention_kernel(lg_ref, cd_ref, q_ref, k_ref, v_ref, g_ref, o_ref, state_ref):
    hd = pl.program_id(1)
    cr = q_ref.shape[1]

    @pl.when(pl.program_id(2) == 0)
    def _():
        state_ref[...] = jnp.zeros(state_ref.shape, F32)

    lg = lg_ref[hd]
    q, k, v = q_ref[0], k_ref[0], v_ref[0]
    diff = (lax.broadcasted_iota(jnp.int32, (cr, cr), 0)
            - lax.broadcasted_iota(jnp.int32, (cr, cr), 1)).astype(F32)
    decay = jnp.where(diff >= 0, jnp.exp(jnp.maximum(diff, 0.0) * lg), 0.0)
    sc = lax.dot_general(q, k, NT_DIMS, preferred_element_type=F32) * decay
    y = jnp.dot(sc.astype(BF16), v, preferred_element_type=F32)
    st = state_ref[...]
    cross = jnp.dot(q, st.astype(BF16), preferred_element_type=F32)
    row_v = lax.broadcasted_iota(jnp.int32, cross.shape, 0).astype(F32)
    y = y + cross * jnp.exp((row_v + 1.0) * lg)
    row_k = lax.broadcasted_iota(jnp.int32, k.shape, 0).astype(F32)
    kd = (k.astype(F32) * jnp.exp((cr - 1.0 - row_k) * lg)).astype(BF16)
    upd = lax.dot_general(kd, v, TN_DIMS, preferred_element_type=F32)
    state_ref[...] = st * cd_ref[hd] + upd

    ms = jnp.mean(y * y, axis=-1, keepdims=True)
    o_ref[0] = (y * lax.rsqrt(ms + EPS) * g_ref[...]).astype(BF16)


def _retention(qkr, vr, g_ret, log_gamma, batch, seq):
    cr = RET_CHUNK
    lg = jnp.asarray(log_gamma, F32)
    cd = jnp.asarray(np.exp(np.float32(cr) * log_gamma).astype(np.float32))
    grid_spec = pltpu.PrefetchScalarGridSpec(
        num_scalar_prefetch=2,
        grid=(batch, RET_HEADS, seq // cr),
        in_specs=[pl.BlockSpec((1, cr, RET_K_DIM), lambda b, h, c, *_: (b, c, h)),
                  pl.BlockSpec((1, cr, RET_K_DIM), lambda b, h, c, *_: (b, c, RET_HEADS + h)),
                  pl.BlockSpec((1, cr, RET_V_DIM), lambda b, h, c, *_: (b, c, h)),
                  pl.BlockSpec((1, RET_V_DIM), lambda b, h, c, *_: (0, 0))],
        out_specs=pl.BlockSpec((1, cr, RET_V_DIM), lambda b, h, c, *_: (b, c, h)),
        scratch_shapes=[pltpu.VMEM((RET_K_DIM, RET_V_DIM), F32)])
    return pl.pallas_call(
        _retention_kernel,
        grid_spec=grid_spec,
        out_shape=jax.ShapeDtypeStruct((batch, seq, D_MODEL), BF16),
        compiler_params=pltpu.CompilerParams(
            dimension_semantics=("arbitrary", "arbitrary", "arbitrary"),
            vmem_limit_bytes=V7X_VMEM_LIMIT),
        name="retention",
    )(lg, cd, qkr, qkr, vr, g_ret)


def _out_mlp_kernel(x_ref, oa_ref, or_ref, ga_ref, gbs_ref, mod_ref, g2_ref,
                    wo_ref, wu_ref, wd_ref, out_ref):
    mod = mod_ref[0]
    u = (ga_ref[...].astype(F32) * oa_ref[...].astype(F32)
         + gbs_ref[...].astype(F32) * or_ref[...].astype(F32)).astype(BF16)
    y = jnp.dot(u, wo_ref[...], preferred_element_type=F32)
    x1 = x_ref[...] + mod[2:3, :] * y
    ms = jnp.mean(x1 * x1, axis=-1, keepdims=True)
    hn = x1 * lax.rsqrt(ms + EPS) * g2_ref[...]
    hff = (hn * (1.0 + mod[4:5, :]) + mod[3:4, :]).astype(BF16)
    up = jnp.dot(hff, wu_ref[...], preferred_element_type=F32)
    act = jnp.square(jnp.maximum(up, 0.0)).astype(BF16)
    ff = jnp.dot(act, wd_ref[...], preferred_element_type=F32)
    out_ref[...] = x1 + mod[5:6, :] * ff


def _out_mlp(x2, oa, orr, ga, gbs, mod3, g2, w_out, w_up, w_down, seq):
    t, d = x2.shape
    tm = TM_MLP
    tpb = seq // tm
    row = pl.BlockSpec((tm, d), lambda i: (i, 0))
    return pl.pallas_call(
        _out_mlp_kernel,
        grid=(t // tm,),
        in_specs=[row, row, row, row, row,
                  pl.BlockSpec((1, N_MOD, d), lambda i: (i // tpb, 0, 0)),
                  pl.BlockSpec((1, d), lambda i: (0, 0)),
                  _resident((d, d)), _resident((d, D_FF)), _resident((D_FF, d))],
        out_specs=row,
        out_shape=jax.ShapeDtypeStruct((t, d), F32),
        compiler_params=pltpu.CompilerParams(
            dimension_semantics=("arbitrary",),
            vmem_limit_bytes=V7X_VMEM_LIMIT),
        name="out_mlp",
    )(x2, oa, orr, ga, gbs, mod3, g2, w_out, w_up, w_down)


def _rope_tables(seq):
    pos = jnp.arange(seq, dtype=F32)[:, None]
    da_inv = ROPE_THETA ** (-jnp.arange(0, DA_QK_DIM, 2, dtype=F32) / DA_QK_DIM)
    ang = pos * da_inv[None, :]
    cos, sin = jnp.cos(ang), jnp.sin(ang)
    zero = jnp.zeros_like(sin)
    cda = jnp.tile(cos, (1, 4))
    sda_a = jnp.tile(jnp.concatenate([-sin, zero], axis=1), (1, 2))
    sda_b = jnp.tile(jnp.concatenate([zero, sin], axis=1), (1, 2))
    ret_inv = 1.0 / (ROPE_THETA ** jnp.linspace(0.0, 1.0, RET_K_DIM // 2, dtype=F32))
    ang_r = pos * ret_inv[None, :]
    cos_r, sin_r = jnp.cos(ang_r), jnp.sin(ang_r)
    cr = jnp.concatenate([cos_r, cos_r], axis=1)
    sr = jnp.concatenate([-sin_r, sin_r], axis=1)
    return cda, sda_a, sda_b, cr, sr


def kernel(x, c, w_ada, b_ada, g_norm1, w_in, g_q, g_k, lambda_q1, lambda_k1,
           lambda_q2, lambda_k2, g_da_out, g_ret_out, w_out, g_norm2, w_up, w_down):
    batch, seq, d = x.shape
    depth = w_ada.shape[0]
    log_gamma = np.log(1.0 - 2.0 ** (-5.0 - np.arange(RET_HEADS))).astype(np.float32)
    tabs = _rope_tables(seq)

    for l in range(depth):
        lambda_init = 0.8 - 0.6 * math.exp(-0.3 * l)
        mod, lam = _modulation(c, w_ada[l], b_ada[l][None, :],
                               lambda_q1[l][None, :], lambda_k1[l][None, :],
                               lambda_q2[l][None, :], lambda_k2[l][None, :],
                               lambda_init)
        mod3 = mod.reshape(batch, N_MOD, d)
        x2 = x.reshape(batch * seq, d)

        w_in_b = w_in[l].astype(BF16)
        w_vt = w_in[l][:, C_VA:C_VA + d].T.astype(BF16)
        gq = (jnp.tile(g_q[l], 2) * (DA_QK_DIM ** -0.5))[None, :]
        gk = jnp.tile(g_k[l], 2)[None, :]
        qa, ka, vt, qkr, vr, ga, gbs = _in_proj(
            x2, mod3, g_norm1[l][None, :], w_in_b, w_vt, gq, gk, tabs, batch, seq)

        g_col = (g_da_out[l] * (1.0 - lambda_init))[:, None]
        oa = _diff_attn(qa.reshape(batch, seq, d), ka.reshape(batch, seq, d), vt,
                        lam, g_col, batch, seq)
        orr = _retention(qkr.reshape(batch, seq, d), vr.reshape(batch, seq, d),
                         g_ret_out[l][None, :], log_gamma, batch, seq)

        out = _out_mlp(x2, oa.reshape(batch * seq, d), orr.reshape(batch * seq, d),
                       ga, gbs, mod3, g_norm2[l][None, :],
                       w_out[l].astype(BF16), w_up[l].astype(BF16),
                       w_down[l].astype(BF16), seq)
        x = out.reshape(batch, seq, d)
    return x
```

```python
import functools
import math

import numpy as np
import jax
import jax.numpy as jnp
from jax import lax
from jax.experimental import pallas as pl
from jax.experimental.pallas import tpu as pltpu

D_MODEL = 1024
CHUNK = 64
ROPE_THETA = 10000.0
EPS = 1e-6
DA_HEADS = 8
DA_V_DIM = D_MODEL // DA_HEADS
DA_QK_DIM = DA_V_DIM // 2
RET_HEADS = 4
RET_V_DIM = D_MODEL // RET_HEADS
RET_K_DIM = RET_V_DIM // 2
D_FF = 4 * D_MODEL
N_MOD = 6
IN_COLS = 8 * D_MODEL

C_QA, C_KA, C_VA, C_QR, C_KR, C_VR, C_GR, C_GA, C_GB = (
    0, 1024, 2048, 3072, 3584, 4096, 5120, 6144, 7168)

LANES = 128
V7X_VMEM_LIMIT = 56 * 1024 * 1024

TM_PROJ = 512
TQ = 512
TK = 512
VT_ROWS = DA_V_DIM + 16
LOG2E = math.log2(math.e)
RET_CHUNK = 256
TM_MLP = 512

F32 = jnp.float32
BF16 = jnp.bfloat16
NT_DIMS = (((1,), (1,)), ((), ()))
TN_DIMS = (((0,), (0,)), ((), ()))


def _resident(shape):
    return pl.BlockSpec(shape, lambda *_: (0,) * len(shape),
                        pipeline_mode=pl.Buffered(1))


def _mod_kernel(c_ref, w_ref, b_ref, lq1_ref, lk1_ref, lq2_ref, lk2_ref,
                mod_ref, lam_ref, *, lambda_init):
    c = c_ref[...]
    sc = (c * jax.nn.sigmoid(c)).astype(BF16)
    mod_ref[...] = jnp.dot(sc, w_ref[...].astype(BF16),
                           preferred_element_type=F32) + b_ref[...]
    s1 = jnp.sum(lq1_ref[...] * lk1_ref[...], axis=-1, keepdims=True)
    s2 = jnp.sum(lq2_ref[...] * lk2_ref[...], axis=-1, keepdims=True)
    lam_ref[...] = jnp.exp(s1) - jnp.exp(s2) + lambda_init


def _modulation(c, w_ada, b_ada, lq1, lk1, lq2, lk2, lambda_init):
    b, d = c.shape
    n = w_ada.shape[1]
    bn = 1536
    vec = pl.BlockSpec((1, DA_QK_DIM), lambda j: (0, 0))
    return pl.pallas_call(
        functools.partial(_mod_kernel, lambda_init=lambda_init),
        grid=(n // bn,),
        in_specs=[pl.BlockSpec((b, d), lambda j: (0, 0)),
                  pl.BlockSpec((d, bn), lambda j: (0, j)),
                  pl.BlockSpec((1, bn), lambda j: (0, j)),
                  vec, vec, vec, vec],
        out_specs=[pl.BlockSpec((b, bn), lambda j: (0, j)),
                   pl.BlockSpec((1, 1), lambda j: (0, 0))],
        out_shape=[jax.ShapeDtypeStruct((b, n), F32),
                   jax.ShapeDtypeStruct((1, 1), F32)],
        compiler_params=pltpu.CompilerParams(
            dimension_semantics=("arbitrary",),
            vmem_limit_bytes=V7X_VMEM_LIMIT),
        name="modulation",
    )(c, w_ada, b_ada, lq1, lk1, lq2, lk2)


def _in_proj_kernel(x_ref, mod_ref, g1_ref, w_ref, wvt_ref, gq_ref, gk_ref,
                    cda_ref, sda_a_ref, sda_b_ref, cr_ref, sr_ref,
                    qa_ref, ka_ref, vt_ref, qkr_ref, vr_ref, ga_ref, gbs_ref):
    tm = x_ref.shape[0]
    x = x_ref[...]
    ms = jnp.mean(x * x, axis=-1, keepdims=True)
    h = x * lax.rsqrt(ms + EPS) * g1_ref[...]
    mod = mod_ref[0]
    hm = (h * (1.0 + mod[1:2, :]) + mod[0:1, :]).astype(BF16)

    def proj(col, width):
        return jnp.dot(hm, w_ref[:, col:col + width], preferred_element_type=F32)

    gi = lax.broadcasted_iota(jnp.int32, (256, 256), 0) // DA_QK_DIM
    gj = lax.broadcasted_iota(jnp.int32, (256, 256), 1) // DA_QK_DIM
    group_sum = jnp.where(gi == gj, 1.0, 0.0).astype(BF16)

    cda, sda_a, sda_b = cda_ref[...], sda_a_ref[...], sda_b_ref[...]

    def qk_norm_rope(acc, gain, out_ref):
        sq = (acc * acc).astype(BF16)
        for blk in range(D_MODEL // 256):
            ssq = jnp.dot(sq[:, blk * 256:(blk + 1) * 256], group_sum,
                          preferred_element_type=F32)
            rs = lax.rsqrt(ssq * (1.0 / DA_QK_DIM) + EPS)
            for half in range(2):
                c0 = blk * 256 + half * LANES
                a = acc[:, c0:c0 + LANES] * rs[:, half * LANES:(half + 1) * LANES] * gain
                rot = (a * cda + pltpu.roll(a, LANES - 32, 1) * sda_a
                       + pltpu.roll(a, 32, 1) * sda_b)
                out_ref[:, c0:c0 + LANES] = rot.astype(BF16)

    qk_norm_rope(proj(C_QA, D_MODEL), gq_ref[...], qa_ref)
    qk_norm_rope(proj(C_KA, D_MODEL), gk_ref[...], ka_ref)

    vt = lax.dot_general(wvt_ref[...], hm, NT_DIMS, preferred_element_type=F32)
    ones = jnp.ones((VT_ROWS - DA_V_DIM, tm), BF16)
    for hd in range(DA_HEADS):
        r0 = hd * VT_ROWS
        vt_ref[0, 0, r0:r0 + DA_V_DIM, :] = (
            vt[hd * DA_V_DIM:(hd + 1) * DA_V_DIM, :].astype(BF16))
        vt_ref[0, 0, r0 + DA_V_DIM:r0 + VT_ROWS, :] = ones

    qkr = proj(C_QR, D_MODEL)
    cr, sr = cr_ref[...], sr_ref[...]
    for j in range(D_MODEL // LANES):
        a = qkr[:, j * LANES:(j + 1) * LANES]
        rot = a * cr + pltpu.roll(a, LANES // 2, 1) * sr
        if j >= RET_HEADS:
            rot = rot * (RET_K_DIM ** -0.5)
        qkr_ref[:, j * LANES:(j + 1) * LANES] = rot.astype(BF16)

    vr_ref[...] = proj(C_VR, D_MODEL).astype(BF16)

    gr = proj(C_GR, D_MODEL)
    ga_ref[...] = jax.nn.sigmoid(proj(C_GA, D_MODEL)).astype(BF16)
    gbs_ref[...] = (jax.nn.sigmoid(proj(C_GB, D_MODEL))
                    * (gr * jax.nn.sigmoid(gr))).astype(BF16)


def _in_proj(x2, mod3, g1, w_in, w_vt, gq, gk, tabs, batch, seq):
    t, d = x2.shape
    tm = TM_PROJ
    tpb = seq // tm
    row = pl.BlockSpec((tm, d), lambda i: (i, 0))
    tab = pl.BlockSpec((tm, LANES), lambda i: (i % tpb, 0))
    vec = pl.BlockSpec((1, LANES), lambda i: (0, 0))
    out_row = jax.ShapeDtypeStruct((t, d), BF16)
    return pl.pallas_call(
        _in_proj_kernel,
        grid=(t // tm,),
        in_specs=[row,
                  pl.BlockSpec((1, N_MOD, d), lambda i: (i // tpb, 0, 0)),
                  pl.BlockSpec((1, d), lambda i: (0, 0)),
                  _resident((d, IN_COLS)),
                  _resident((d, d)),
                  vec, vec, tab, tab, tab, tab, tab],
        out_specs=[row, row,
                   pl.BlockSpec((1, 1, DA_HEADS * VT_ROWS, tm),
                                lambda i: (i // tpb, (i % tpb) // (TK // tm), 0,
                                           (i % tpb) % (TK // tm))),
                   row, row, row, row],
        out_shape=[out_row, out_row,
                   jax.ShapeDtypeStruct((batch, seq // TK, DA_HEADS * VT_ROWS, TK), BF16),
                   out_row, out_row, out_row, out_row],
        compiler_params=pltpu.CompilerParams(
            dimension_semantics=("arbitrary",),
            vmem_limit_bytes=V7X_VMEM_LIMIT),
        name="in_proj",
    )(x2, mod3, g1, w_in, w_vt, gq, gk, *tabs)


def _diff_attn_kernel(q_ref, k_ref, vt_ref, lam_ref, g_ref, o_ref,
                      sa_ref, sb_ref, acc_ref, m_ref):
    qi = pl.program_id(2)
    q = q_ref[0]
    lane = lax.broadcasted_iota(jnp.int32, q.shape, 1)
    zero = jnp.zeros_like(q)
    qcat = jnp.concatenate([jnp.where(lane < DA_QK_DIM, q, zero),
                            jnp.where(lane >= DA_QK_DIM, q, zero)], axis=0)

    m_ref[...] = jnp.full(m_ref.shape, -jnp.inf, F32)
    acc_ref[...] = jnp.zeros(acc_ref.shape, F32)

    def scores(j, s_ref):
        koff = pl.multiple_of(j * TK, TK)
        s_ref[...] = lax.dot_general(k_ref[0, pl.ds(koff, TK), :], qcat, NT_DIMS,
                                     preferred_element_type=F32)

    def softmax_pv(j, s_ref, masked):
        s = s_ref[...]
        if masked:
            kc = lax.broadcasted_iota(jnp.int32, s.shape, 0) // CHUNK
            qc = (lax.broadcasted_iota(jnp.int32, s.shape, 1) % TQ) // CHUNK
            s = jnp.where(kc <= qc, s, -jnp.inf)
        m_old = m_ref[...]
        m_new = jnp.maximum(m_old, jnp.max(s, axis=0, keepdims=True))
        alpha = jnp.exp2(m_old - m_new)
        e = jnp.exp2(s - m_new).astype(BF16)
        pv = jnp.dot(vt_ref[0, j], e, preferred_element_type=F32)
        acc_ref[...] = alpha * acc_ref[...] + pv
        m_ref[...] = m_new

    scores(0, sa_ref)

    def pair(t, carry):
        scores(2 * t + 1, sb_ref)
        softmax_pv(2 * t, sa_ref, False)
        scores(2 * t + 2, sa_ref)
        softmax_pv(2 * t + 1, sb_ref, False)
        return carry

    lax.fori_loop(0, qi // 2, pair, 0)

    @pl.when(qi % 2 == 0)
    def _():
        softmax_pv(qi, sa_ref, True)

    @pl.when(qi % 2 == 1)
    def _():
        scores(qi, sb_ref)
        softmax_pv(qi - 1, sa_ref, False)
        softmax_pv(qi, sb_ref, True)

    acc = acc_ref[...]
    num = acc[:DA_V_DIM, :]
    l = acc[DA_V_DIM:DA_V_DIM + 1, :]
    o = num[:, :TQ] / l[:, :TQ] - lam_ref[...] * (num[:, TQ:] / l[:, TQ:])
    ms = jnp.mean(o * o, axis=0, keepdims=True)
    on = o * lax.rsqrt(ms + EPS) * g_ref[...]
    o_ref[0] = on.T.astype(BF16)


def _diff_attn(qa, ka, vt, lam, g_col, batch, seq):
    nq = seq // TQ
    return pl.pallas_call(
        _diff_attn_kernel,
        grid=(batch, DA_HEADS, nq),
        in_specs=[pl.BlockSpec((1, TQ, LANES), lambda b, h, i: (b, i, h)),
                  pl.BlockSpec((1, seq, LANES), lambda b, h, i: (b, 0, h)),
                  pl.BlockSpec((1, seq // TK, VT_ROWS, TK), lambda b, h, i: (b, 0, h, 0)),
                  pl.BlockSpec((1, 1), lambda b, h, i: (0, 0)),
                  pl.BlockSpec((DA_V_DIM, 1), lambda b, h, i: (0, 0))],
        out_specs=pl.BlockSpec((1, TQ, LANES), lambda b, h, i: (b, i, h)),
        out_shape=jax.ShapeDtypeStruct((batch, seq, D_MODEL), BF16),
        scratch_shapes=[pltpu.VMEM((TK, 2 * TQ), F32),
                        pltpu.VMEM((TK, 2 * TQ), F32),
                        pltpu.VMEM((VT_ROWS, 2 * TQ), F32),
                        pltpu.VMEM((1, 2 * TQ), F32)],
        compiler_params=pltpu.CompilerParams(
            dimension_semantics=("arbitrary", "arbitrary", "arbitrary"),
            vmem_limit_bytes=V7X_VMEM_LIMIT),
        name="diff_attn",
    )(qa, ka, vt, lam, g_col)


def _retention_kernel(lg_ref, cd_ref, q_ref, k_ref, v_ref, g_ref, o_ref, state_ref):
    hd = pl.program_id(1)
    cr = q_ref.shape[1]

    @pl.when(pl.program_id(2) == 0)
    def _():
        state_ref[...] = jnp.zeros(state_ref.shape, F32)

    lg = lg_ref[hd]
    q, k, v = q_ref[0], k_ref[0], v_ref[0]
    diff = (lax.broadcasted_iota(jnp.int32, (cr, cr), 0)
            - lax.broadcasted_iota(jnp.int32, (cr, cr), 1)).astype(F32)
    decay = jnp.where(diff >= 0, jnp.exp(jnp.maximum(diff, 0.0) * lg), 0.0)
    sc = lax.dot_general(q, k, NT_DIMS, preferred_element_type=F32) * decay
    y = jnp.dot(sc.astype(BF16), v, preferred_element_type=F32)
    st = state_ref[...]
    cross = jnp.dot(q, st.astype(BF16), preferred_element_type=F32)
    row_v = lax.broadcasted_iota(jnp.int32, cross.shape, 0).astype(F32)
    y = y + cross * jnp.exp((row_v + 1.0) * lg)
    row_k = lax.broadcasted_iota(jnp.int32, k.shape, 0).astype(F32)
    kd = (k.astype(F32) * jnp.exp((cr - 1.0 - row_k) * lg)).astype(BF16)
    upd = lax.dot_general(kd, v, TN_DIMS, preferred_element_type=F32)
    state_ref[...] = st * cd_ref[hd] + upd

    ms = jnp.mean(y * y, axis=-1, keepdims=True)
    o_ref[0] = (y * lax.rsqrt(ms + EPS) * g_ref[...]).astype(BF16)


def _retention(qkr, vr, g_ret, log_gamma, batch, seq):
    cr = RET_CHUNK
    lg = jnp.asarray(log_gamma, F32)
    cd = jnp.asarray(np.exp(np.float32(cr) * log_gamma).astype(np.float32))
    grid_spec = pltpu.PrefetchScalarGridSpec(
        num_scalar_prefetch=2,
        grid=(batch, RET_HEADS, seq // cr),
        in_specs=[pl.BlockSpec((1, cr, RET_K_DIM), lambda b, h, c, *_: (b, c, h)),
                  pl.BlockSpec((1, cr, RET_K_DIM), lambda b, h, c, *_: (b, c, RET_HEADS + h)),
                  pl.BlockSpec((1, cr, RET_V_DIM), lambda b, h, c, *_: (b, c, h)),
                  pl.BlockSpec((1, RET_V_DIM), lambda b, h, c, *_: (0, 0))],
        out_specs=pl.BlockSpec((1, cr, RET_V_DIM), lambda b, h, c, *_: (b, c, h)),
        scratch_shapes=[pltpu.VMEM((RET_K_DIM, RET_V_DIM), F32)])
    return pl.pallas_call(
        _retention_kernel,
        grid_spec=grid_spec,
        out_shape=jax.ShapeDtypeStruct((batch, seq, D_MODEL), BF16),
        compiler_params=pltpu.CompilerParams(
            dimension_semantics=("arbitrary", "arbitrary", "arbitrary"),
            vmem_limit_bytes=V7X_VMEM_LIMIT),
        name="retention",
    )(lg, cd, qkr, qkr, vr, g_ret)


def _out_mlp_kernel(x_ref, oa_ref, or_ref, ga_ref, gbs_ref, mod_ref, g2_ref,
                    wo_ref, wu_ref, wd_ref, out_ref):
    mod = mod_ref[0]
    u = (ga_ref[...].astype(F32) * oa_ref[...].astype(F32)
         + gbs_ref[...].astype(F32) * or_ref[...].astype(F32)).astype(BF16)
    y = jnp.dot(u, wo_ref[...], preferred_element_type=F32)
    x1 = x_ref[...] + mod[2:3, :] * y
    ms = jnp.mean(x1 * x1, axis=-1, keepdims=True)
    hn = x1 * lax.rsqrt(ms + EPS) * g2_ref[...]
    hff = (hn * (1.0 + mod[4:5, :]) + mod[3:4, :]).astype(BF16)
    up = jnp.dot(hff, wu_ref[...], preferred_element_type=F32)
    act = jnp.square(jnp.maximum(up, 0.0)).astype(BF16)
    ff = jnp.dot(act, wd_ref[...], preferred_element_type=F32)
    out_ref[...] = x1 + mod[5:6, :] * ff


def _out_mlp(x2, oa, orr, ga, gbs, mod3, g2, w_out, w_up, w_down, seq):
    t, d = x2.shape
    tm = TM_MLP
    tpb = seq // tm
    row = pl.BlockSpec((tm, d), lambda i: (i, 0))
    return pl.pallas_call(
        _out_mlp_kernel,
        grid=(t // tm,),
        in_specs=[row, row, row, row, row,
                  pl.BlockSpec((1, N_MOD, d), lambda i: (i // tpb, 0, 0)),
                  pl.BlockSpec((1, d), lambda i: (0, 0)),
                  _resident((d, d)), _resident((d, D_FF)), _resident((D_FF, d))],
        out_specs=row,
        out_shape=jax.ShapeDtypeStruct((t, d), F32),
        compiler_params=pltpu.CompilerParams(
            dimension_semantics=("arbitrary",),
            vmem_limit_bytes=V7X_VMEM_LIMIT),
        name="out_mlp",
    )(x2, oa, orr, ga, gbs, mod3, g2, w_out, w_up, w_down)


def _rope_tables(seq):
    pos = jnp.arange(seq, dtype=F32)[:, None]
    da_inv = ROPE_THETA ** (-jnp.arange(0, DA_QK_DIM, 2, dtype=F32) / DA_QK_DIM)
    ang = pos * da_inv[None, :]
    cos, sin = jnp.cos(ang), jnp.sin(ang)
    zero = jnp.zeros_like(sin)
    cda = jnp.tile(cos, (1, 4))
    sda_a = jnp.tile(jnp.concatenate([-sin, zero], axis=1), (1, 2))
    sda_b = jnp.tile(jnp.concatenate([zero, sin], axis=1), (1, 2))
    ret_inv = 1.0 / (ROPE_THETA ** jnp.linspace(0.0, 1.0, RET_K_DIM // 2, dtype=F32))
    ang_r = pos * ret_inv[None, :]
    cos_r, sin_r = jnp.cos(ang_r), jnp.sin(ang_r)
    cr = jnp.concatenate([cos_r, cos_r], axis=1)
    sr = jnp.concatenate([-sin_r, sin_r], axis=1)
    return cda, sda_a, sda_b, cr, sr


def kernel(x, c, w_ada, b_ada, g_norm1, w_in, g_q, g_k, lambda_q1, lambda_k1,
           lambda_q2, lambda_k2, g_da_out, g_ret_out, w_out, g_norm2, w_up, w_down):
    batch, seq, d = x.shape
    depth = w_ada.shape[0]
    log_gamma = np.log(1.0 - 2.0 ** (-5.0 - np.arange(RET_HEADS))).astype(np.float32)
    tabs = _rope_tables(seq)

    for l in range(depth):
        lambda_init = 0.8 - 0.6 * math.exp(-0.3 * l)
        mod, lam = _modulation(c, w_ada[l], b_ada[l][None, :],
                               lambda_q1[l][None, :], lambda_k1[l][None, :],
                               lambda_q2[l][None, :], lambda_k2[l][None, :],
                               lambda_init)
        mod3 = mod.reshape(batch, N_MOD, d)
        x2 = x.reshape(batch * seq, d)

        w_in_b = w_in[l].astype(BF16)
        w_vt = w_in[l][:, C_VA:C_VA + d].T.astype(BF16)
        gq = (jnp.tile(g_q[l], 2) * (DA_QK_DIM ** -0.5 * LOG2E))[None, :]
        gk = jnp.tile(g_k[l], 2)[None, :]
        qa, ka, vt, qkr, vr, ga, gbs = _in_proj(
            x2, mod3, g_norm1[l][None, :], w_in_b, w_vt, gq, gk, tabs, batch, seq)

        g_col = (g_da_out[l] * (1.0 - lambda_init))[:, None]
        oa = _diff_attn(qa.reshape(batch, seq, d), ka.reshape(batch, seq, d), vt,
                        lam, g_col, batch, seq)
        orr = _retention(qkr.reshape(batch, seq, d), vr.reshape(batch, seq, d),
                         g_ret_out[l][None, :], log_gamma, batch, seq)

        out = _out_mlp(x2, oa.reshape(batch * seq, d), orr.reshape(batch * seq, d),
                       ga, gbs, mod3, g_norm2[l][None, :],
                       w_out[l].astype(BF16), w_up[l].astype(BF16),
                       w_down[l].astype(BF16), seq)
        x = out.reshape(batch, seq, d)
    return x
```

```python
import functools
import math

import numpy as np
import jax
import jax.numpy as jnp
from jax import lax
from jax.experimental import pallas as pl
from jax.experimental.pallas import tpu as pltpu

D_MODEL = 1024
CHUNK = 64
ROPE_THETA = 10000.0
EPS = 1e-6
DA_HEADS = 8
DA_V_DIM = D_MODEL // DA_HEADS
DA_QK_DIM = DA_V_DIM // 2
RET_HEADS = 4
RET_V_DIM = D_MODEL // RET_HEADS
RET_K_DIM = RET_V_DIM // 2
D_FF = 4 * D_MODEL
N_MOD = 6
IN_COLS = 8 * D_MODEL

C_QA, C_KA, C_VA, C_QR, C_KR, C_VR, C_GR, C_GA, C_GB = (
    0, 1024, 2048, 3072, 3584, 4096, 5120, 6144, 7168)

LANES = 128
V7X_VMEM_LIMIT = 56 * 1024 * 1024

TM_PROJ = 512
TQ = 512
TK = 512
ATTN_HEADS_PER_STEP = 2
VT_ROWS = DA_V_DIM + 16
LOG2E = math.log2(math.e)
RET_CHUNK = 256
TM_MLP = 512

F32 = jnp.float32
BF16 = jnp.bfloat16
NT_DIMS = (((1,), (1,)), ((), ()))
TN_DIMS = (((0,), (0,)), ((), ()))


def _resident(shape):
    return pl.BlockSpec(shape, lambda *_: (0,) * len(shape),
                        pipeline_mode=pl.Buffered(1))


def _mod_kernel(c_ref, w_ref, b_ref, lq1_ref, lk1_ref, lq2_ref, lk2_ref,
                mod_ref, lam_ref, *, lambda_init):
    c = c_ref[...]
    sc = (c * jax.nn.sigmoid(c)).astype(BF16)
    mod_ref[...] = jnp.dot(sc, w_ref[...].astype(BF16),
                           preferred_element_type=F32) + b_ref[...]
    s1 = jnp.sum(lq1_ref[...] * lk1_ref[...], axis=-1, keepdims=True)
    s2 = jnp.sum(lq2_ref[...] * lk2_ref[...], axis=-1, keepdims=True)
    lam_ref[...] = jnp.exp(s1) - jnp.exp(s2) + lambda_init


def _modulation(c, w_ada, b_ada, lq1, lk1, lq2, lk2, lambda_init):
    b, d = c.shape
    n = w_ada.shape[1]
    bn = 1536
    vec = pl.BlockSpec((1, DA_QK_DIM), lambda j: (0, 0))
    return pl.pallas_call(
        functools.partial(_mod_kernel, lambda_init=lambda_init),
        grid=(n // bn,),
        in_specs=[pl.BlockSpec((b, d), lambda j: (0, 0)),
                  pl.BlockSpec((d, bn), lambda j: (0, j)),
                  pl.BlockSpec((1, bn), lambda j: (0, j)),
                  vec, vec, vec, vec],
        out_specs=[pl.BlockSpec((b, bn), lambda j: (0, j)),
                   pl.BlockSpec((1, 1), lambda j: (0, 0))],
        out_shape=[jax.ShapeDtypeStruct((b, n), F32),
                   jax.ShapeDtypeStruct((1, 1), F32)],
        compiler_params=pltpu.CompilerParams(
            dimension_semantics=("arbitrary",),
            vmem_limit_bytes=V7X_VMEM_LIMIT),
        name="modulation",
    )(c, w_ada, b_ada, lq1, lk1, lq2, lk2)


def _in_proj_kernel(x_ref, mod_ref, g1_ref, w_ref, wvt_ref, gq_ref, gk_ref,
                    cda_ref, sda_a_ref, sda_b_ref, cr_ref, sr_ref,
                    qa_ref, ka_ref, vt_ref, qkr_ref, vr_ref, ga_ref, gbs_ref):
    tm = x_ref.shape[0]
    x = x_ref[...]
    ms = jnp.mean(x * x, axis=-1, keepdims=True)
    h = x * lax.rsqrt(ms + EPS) * g1_ref[...]
    mod = mod_ref[0]
    hm = (h * (1.0 + mod[1:2, :]) + mod[0:1, :]).astype(BF16)

    def proj(col, width):
        return jnp.dot(hm, w_ref[:, col:col + width], preferred_element_type=F32)

    gi = lax.broadcasted_iota(jnp.int32, (256, 256), 0) // DA_QK_DIM
    gj = lax.broadcasted_iota(jnp.int32, (256, 256), 1) // DA_QK_DIM
    group_sum = jnp.where(gi == gj, 1.0, 0.0).astype(BF16)

    cda, sda_a, sda_b = cda_ref[...], sda_a_ref[...], sda_b_ref[...]

    def qk_norm_rope(acc, gain, out_ref):
        sq = (acc * acc).astype(BF16)
        for blk in range(D_MODEL // 256):
            ssq = jnp.dot(sq[:, blk * 256:(blk + 1) * 256], group_sum,
                          preferred_element_type=F32)
            rs = lax.rsqrt(ssq * (1.0 / DA_QK_DIM) + EPS)
            for half in range(2):
                c0 = blk * 256 + half * LANES
                a = acc[:, c0:c0 + LANES] * rs[:, half * LANES:(half + 1) * LANES] * gain
                rot = (a * cda + pltpu.roll(a, LANES - 32, 1) * sda_a
                       + pltpu.roll(a, 32, 1) * sda_b)
                out_ref[:, c0:c0 + LANES] = rot.astype(BF16)

    qk_norm_rope(proj(C_QA, D_MODEL), gq_ref[...], qa_ref)
    qk_norm_rope(proj(C_KA, D_MODEL), gk_ref[...], ka_ref)

    vt = lax.dot_general(wvt_ref[...], hm, NT_DIMS, preferred_element_type=F32)
    ones = jnp.ones((VT_ROWS - DA_V_DIM, tm), BF16)
    for hd in range(DA_HEADS):
        r0 = hd * VT_ROWS
        vt_ref[0, 0, r0:r0 + DA_V_DIM, :] = (
            vt[hd * DA_V_DIM:(hd + 1) * DA_V_DIM, :].astype(BF16))
        vt_ref[0, 0, r0 + DA_V_DIM:r0 + VT_ROWS, :] = ones

    qkr = proj(C_QR, D_MODEL)
    cr, sr = cr_ref[...], sr_ref[...]
    for j in range(D_MODEL // LANES):
        a = qkr[:, j * LANES:(j + 1) * LANES]
        rot = a * cr + pltpu.roll(a, LANES // 2, 1) * sr
        if j >= RET_HEADS:
            rot = rot * (RET_K_DIM ** -0.5)
        qkr_ref[:, j * LANES:(j + 1) * LANES] = rot.astype(BF16)

    vr_ref[...] = proj(C_VR, D_MODEL).astype(BF16)

    gr = proj(C_GR, D_MODEL)
    ga_ref[...] = jax.nn.sigmoid(proj(C_GA, D_MODEL)).astype(BF16)
    gbs_ref[...] = (jax.nn.sigmoid(proj(C_GB, D_MODEL))
                    * (gr * jax.nn.sigmoid(gr))).astype(BF16)


def _in_proj(x2, mod3, g1, w_in, w_vt, gq, gk, tabs, batch, seq):
    t, d = x2.shape
    tm = TM_PROJ
    tpb = seq // tm
    row = pl.BlockSpec((tm, d), lambda i: (i, 0))
    tab = pl.BlockSpec((tm, LANES), lambda i: (i % tpb, 0))
    vec = pl.BlockSpec((1, LANES), lambda i: (0, 0))
    out_row = jax.ShapeDtypeStruct((t, d), BF16)
    return pl.pallas_call(
        _in_proj_kernel,
        grid=(t // tm,),
        in_specs=[row,
                  pl.BlockSpec((1, N_MOD, d), lambda i: (i // tpb, 0, 0)),
                  pl.BlockSpec((1, d), lambda i: (0, 0)),
                  _resident((d, IN_COLS)),
                  _resident((d, d)),
                  vec, vec, tab, tab, tab, tab, tab],
        out_specs=[row, row,
                   pl.BlockSpec((1, 1, DA_HEADS * VT_ROWS, tm),
                                lambda i: (i // tpb, (i % tpb) // (TK // tm), 0,
                                           (i % tpb) % (TK // tm))),
                   row, row, row, row],
        out_shape=[out_row, out_row,
                   jax.ShapeDtypeStruct((batch, seq // TK, DA_HEADS * VT_ROWS, TK), BF16),
                   out_row, out_row, out_row, out_row],
        compiler_params=pltpu.CompilerParams(
            dimension_semantics=("arbitrary",),
            vmem_limit_bytes=V7X_VMEM_LIMIT),
        name="in_proj",
    )(x2, mod3, g1, w_in, w_vt, gq, gk, *tabs)


def _diff_attn_kernel(q_ref, k_ref, vt_ref, lam_ref, g_ref, o_ref,
                      sa_ref, sb_ref, acc_ref, m_ref):
    qi = pl.program_id(2)
    heads = range(ATTN_HEADS_PER_STEP)
    lane = lax.broadcasted_iota(jnp.int32, (TQ, LANES), 1)
    qcat = []
    for hd in heads:
        q = q_ref[0, :, hd * LANES:(hd + 1) * LANES]
        zero = jnp.zeros_like(q)
        qcat.append(jnp.concatenate([jnp.where(lane < DA_QK_DIM, q, zero),
                                     jnp.where(lane >= DA_QK_DIM, q, zero)], axis=0))

    m_ref[...] = jnp.full(m_ref.shape, -jnp.inf, F32)
    acc_ref[...] = jnp.zeros(acc_ref.shape, F32)

    def scores(j, s_ref):
        koff = pl.multiple_of(j * TK, TK)
        for hd in heads:
            s_ref[hd] = lax.dot_general(
                k_ref[0, pl.ds(koff, TK), hd * LANES:(hd + 1) * LANES], qcat[hd], NT_DIMS,
                preferred_element_type=F32)

    def softmax_pv(j, s_ref, masked):
        for hd in heads:
            s = s_ref[hd]
            if masked:
                kc = lax.broadcasted_iota(jnp.int32, s.shape, 0) // CHUNK
                qc = (lax.broadcasted_iota(jnp.int32, s.shape, 1) % TQ) // CHUNK
                s = jnp.where(kc <= qc, s, -jnp.inf)
            m_old = m_ref[hd]
            m_new = jnp.maximum(m_old, jnp.max(s, axis=0, keepdims=True))
            alpha = jnp.exp2(m_old - m_new)
            e = jnp.exp2(s - m_new).astype(BF16)
            pv = jnp.dot(vt_ref[0, j, hd * VT_ROWS:(hd + 1) * VT_ROWS, :], e,
                         preferred_element_type=F32)
            acc_ref[hd] = alpha * acc_ref[hd] + pv
            m_ref[hd] = m_new

    scores(0, sa_ref)

    def pair(t, carry):
        scores(2 * t + 1, sb_ref)
        softmax_pv(2 * t, sa_ref, False)
        scores(2 * t + 2, sa_ref)
        softmax_pv(2 * t + 1, sb_ref, False)
        return carry

    lax.fori_loop(0, qi // 2, pair, 0)

    @pl.when(qi % 2 == 0)
    def _():
        softmax_pv(qi, sa_ref, True)

    @pl.when(qi % 2 == 1)
    def _():
        scores(qi, sb_ref)
        softmax_pv(qi - 1, sa_ref, False)
        softmax_pv(qi, sb_ref, True)

    for hd in heads:
        acc = acc_ref[hd]
        num = acc[:DA_V_DIM, :]
        l = acc[DA_V_DIM:DA_V_DIM + 1, :]
        o = num[:, :TQ] / l[:, :TQ] - lam_ref[...] * (num[:, TQ:] / l[:, TQ:])
        ms = jnp.mean(o * o, axis=0, keepdims=True)
        on = o * lax.rsqrt(ms + EPS) * g_ref[...]
        o_ref[0, :, hd * LANES:(hd + 1) * LANES] = on.T.astype(BF16)


def _diff_attn(qa, ka, vt, lam, g_col, batch, seq):
    hps = ATTN_HEADS_PER_STEP
    nq = seq // TQ
    return pl.pallas_call(
        _diff_attn_kernel,
        grid=(batch, DA_HEADS // hps, nq),
        in_specs=[pl.BlockSpec((1, TQ, hps * LANES), lambda b, h, i: (b, i, h)),
                  pl.BlockSpec((1, seq, hps * LANES), lambda b, h, i: (b, 0, h)),
                  pl.BlockSpec((1, seq // TK, hps * VT_ROWS, TK), lambda b, h, i: (b, 0, h, 0)),
                  pl.BlockSpec((1, 1), lambda b, h, i: (0, 0)),
                  pl.BlockSpec((DA_V_DIM, 1), lambda b, h, i: (0, 0))],
        out_specs=pl.BlockSpec((1, TQ, hps * LANES), lambda b, h, i: (b, i, h)),
        out_shape=jax.ShapeDtypeStruct((batch, seq, D_MODEL), BF16),
        scratch_shapes=[pltpu.VMEM((hps, TK, 2 * TQ), F32),
                        pltpu.VMEM((hps, TK, 2 * TQ), F32),
                        pltpu.VMEM((hps, VT_ROWS, 2 * TQ), F32),
                        pltpu.VMEM((hps, 1, 2 * TQ), F32)],
        compiler_params=pltpu.CompilerParams(
            dimension_semantics=("arbitrary", "arbitrary", "arbitrary"),
            vmem_limit_bytes=V7X_VMEM_LIMIT),
        name="diff_attn",
    )(qa, ka, vt, lam, g_col)


def _retention_kernel(qk_ref, v_ref, g_ref, intra_ref, inner_ref, kvd_ref, o_ref,
                      state_ref, *, chunk_decay):
    @pl.when(pl.program_id(1) == 0)
    def _():
        state_ref[...] = jnp.zeros(state_ref.shape, F32)

    for hd in range(RET_HEADS):
        q = qk_ref[0, :, hd * RET_K_DIM:(hd + 1) * RET_K_DIM]
        k = qk_ref[0, :, (RET_HEADS + hd) * RET_K_DIM:(RET_HEADS + hd + 1) * RET_K_DIM]
        v = v_ref[0, :, hd * RET_V_DIM:(hd + 1) * RET_V_DIM]
        sc = lax.dot_general(q, k, NT_DIMS, preferred_element_type=F32) * intra_ref[hd]
        y = jnp.dot(sc.astype(BF16), v, preferred_element_type=F32)
        st = state_ref[hd]
        cross = jnp.dot(q, st.astype(BF16), preferred_element_type=F32)
        y = y + cross * inner_ref[hd]
        kd = (k.astype(F32) * kvd_ref[hd]).astype(BF16)
        upd = lax.dot_general(kd, v, TN_DIMS, preferred_element_type=F32)
        state_ref[hd] = st * chunk_decay[hd] + upd
        ms = jnp.mean(y * y, axis=-1, keepdims=True)
        o_ref[0, :, hd * RET_V_DIM:(hd + 1) * RET_V_DIM] = (
            y * lax.rsqrt(ms + EPS) * g_ref[...]).astype(BF16)


def _retention_tables(log_gamma, cr):
    n = np.arange(cr, dtype=np.float32)
    lg = log_gamma.astype(np.float32)[:, None, None]
    diff = n[:, None] - n[None, :]
    intra = np.where(diff >= 0, np.exp(np.maximum(diff, 0.0) * lg), 0.0).astype(np.float32)
    inner = np.exp((n + 1.0)[None, :, None] * lg).astype(np.float32)
    kvd = np.exp((cr - 1.0 - n)[None, :, None] * lg).astype(np.float32)
    inner = np.broadcast_to(inner, (RET_HEADS, cr, RET_V_DIM))
    kvd = np.broadcast_to(kvd, (RET_HEADS, cr, RET_K_DIM))
    chunk = tuple(float(np.exp(np.float32(cr) * g)) for g in log_gamma.astype(np.float32))
    return jnp.asarray(intra), jnp.asarray(inner), jnp.asarray(kvd), chunk


def _retention(qkr, vr, g_ret, log_gamma, batch, seq):
    cr = RET_CHUNK
    intra, inner, kvd, chunk_decay = _retention_tables(log_gamma, cr)
    rows = pl.BlockSpec((1, cr, D_MODEL), lambda b, c: (b, c, 0))
    return pl.pallas_call(
        functools.partial(_retention_kernel, chunk_decay=chunk_decay),
        grid=(batch, seq // cr),
        in_specs=[rows, rows,
                  pl.BlockSpec((1, RET_V_DIM), lambda b, c: (0, 0)),
                  _resident((RET_HEADS, cr, cr)),
                  _resident((RET_HEADS, cr, RET_V_DIM)),
                  _resident((RET_HEADS, cr, RET_K_DIM))],
        out_specs=rows,
        out_shape=jax.ShapeDtypeStruct((batch, seq, D_MODEL), BF16),
        scratch_shapes=[pltpu.VMEM((RET_HEADS, RET_K_DIM, RET_V_DIM), F32)],
        compiler_params=pltpu.CompilerParams(
            dimension_semantics=("arbitrary", "arbitrary"),
            vmem_limit_bytes=V7X_VMEM_LIMIT),
        name="retention",
    )(qkr, vr, g_ret, intra, inner, kvd)


def _out_mlp_kernel(x_ref, oa_ref, or_ref, ga_ref, gbs_ref, mod_ref, g2_ref,
                    wo_ref, wu_ref, wd_ref, out_ref):
    mod = mod_ref[0]
    u = (ga_ref[...].astype(F32) * oa_ref[...].astype(F32)
         + gbs_ref[...].astype(F32) * or_ref[...].astype(F32)).astype(BF16)
    y = jnp.dot(u, wo_ref[...], preferred_element_type=F32)
    x1 = x_ref[...] + mod[2:3, :] * y
    ms = jnp.mean(x1 * x1, axis=-1, keepdims=True)
    hn = x1 * lax.rsqrt(ms + EPS) * g2_ref[...]
    hff = (hn * (1.0 + mod[4:5, :]) + mod[3:4, :]).astype(BF16)
    up = jnp.dot(hff, wu_ref[...], preferred_element_type=F32)
    act = jnp.square(jnp.maximum(up, 0.0)).astype(BF16)
    ff = jnp.dot(act, wd_ref[...], preferred_element_type=F32)
    out_ref[...] = x1 + mod[5:6, :] * ff


def _out_mlp(x2, oa, orr, ga, gbs, mod3, g2, w_out, w_up, w_down, seq):
    t, d = x2.shape
    tm = TM_MLP
    tpb = seq // tm
    row = pl.BlockSpec((tm, d), lambda i: (i, 0))
    return pl.pallas_call(
        _out_mlp_kernel,
        grid=(t // tm,),
        in_specs=[row, row, row, row, row,
                  pl.BlockSpec((1, N_MOD, d), lambda i: (i // tpb, 0, 0)),
                  pl.BlockSpec((1, d), lambda i: (0, 0)),
                  _resident((d, d)), _resident((d, D_FF)), _resident((D_FF, d))],
        out_specs=row,
        out_shape=jax.ShapeDtypeStruct((t, d), F32),
        compiler_params=pltpu.CompilerParams(
            dimension_semantics=("arbitrary",),
            vmem_limit_bytes=V7X_VMEM_LIMIT),
        name="out_mlp",
    )(x2, oa, orr, ga, gbs, mod3, g2, w_out, w_up, w_down)


def _rope_tables(seq):
    pos = jnp.arange(seq, dtype=F32)[:, None]
    da_inv = ROPE_THETA ** (-jnp.arange(0, DA_QK_DIM, 2, dtype=F32) / DA_QK_DIM)
    ang = pos * da_inv[None, :]
    cos, sin = jnp.cos(ang), jnp.sin(ang)
    zero = jnp.zeros_like(sin)
    cda = jnp.tile(cos, (1, 4))
    sda_a = jnp.tile(jnp.concatenate([-sin, zero], axis=1), (1, 2))
    sda_b = jnp.tile(jnp.concatenate([zero, sin], axis=1), (1, 2))
    ret_inv = 1.0 / (ROPE_THETA ** jnp.linspace(0.0, 1.0, RET_K_DIM // 2, dtype=F32))
    ang_r = pos * ret_inv[None, :]
    cos_r, sin_r = jnp.cos(ang_r), jnp.sin(ang_r)
    cr = jnp.concatenate([cos_r, cos_r], axis=1)
    sr = jnp.concatenate([-sin_r, sin_r], axis=1)
    return cda, sda_a, sda_b, cr, sr


def kernel(x, c, w_ada, b_ada, g_norm1, w_in, g_q, g_k, lambda_q1, lambda_k1,
           lambda_q2, lambda_k2, g_da_out, g_ret_out, w_out, g_norm2, w_up, w_down):
    batch, seq, d = x.shape
    depth = w_ada.shape[0]
    log_gamma = np.log(1.0 - 2.0 ** (-5.0 - np.arange(RET_HEADS))).astype(np.float32)
    tabs = _rope_tables(seq)

    for l in range(depth):
        lambda_init = 0.8 - 0.6 * math.exp(-0.3 * l)
        mod, lam = _modulation(c, w_ada[l], b_ada[l][None, :],
                               lambda_q1[l][None, :], lambda_k1[l][None, :],
                               lambda_q2[l][None, :], lambda_k2[l][None, :],
                               lambda_init)
        mod3 = mod.reshape(batch, N_MOD, d)
        x2 = x.reshape(batch * seq, d)

        w_in_b = w_in[l].astype(BF16)
        w_vt = w_in[l][:, C_VA:C_VA + d].T.astype(BF16)
        gq = (jnp.tile(g_q[l], 2) * (DA_QK_DIM ** -0.5 * LOG2E))[None, :]
        gk = jnp.tile(g_k[l], 2)[None, :]
        qa, ka, vt, qkr, vr, ga, gbs = _in_proj(
            x2, mod3, g_norm1[l][None, :], w_in_b, w_vt, gq, gk, tabs, batch, seq)

        g_col = (g_da_out[l] * (1.0 - lambda_init))[:, None]
        oa = _diff_attn(qa.reshape(batch, seq, d), ka.reshape(batch, seq, d), vt,
                        lam, g_col, batch, seq)
        orr = _retention(qkr.reshape(batch, seq, d), vr.reshape(batch, seq, d),
                         g_ret_out[l][None, :], log_gamma, batch, seq)

        out = _out_mlp(x2, oa.reshape(batch * seq, d), orr.reshape(batch * seq, d),
                       ga, gbs, mod3, g_norm2[l][None, :],
                       w_out[l].astype(BF16), w_up[l].astype(BF16),
                       w_down[l].astype(BF16), seq)
        x = out.reshape(batch, seq, d)
    return x
```

```python
import functools
import math

import numpy as np
import jax
import jax.numpy as jnp
from jax import lax
from jax.experimental import pallas as pl
from jax.experimental.pallas import tpu as pltpu

D_MODEL = 1024
CHUNK = 64
ROPE_THETA = 10000.0
EPS = 1e-6
DA_HEADS = 8
DA_V_DIM = D_MODEL // DA_HEADS
DA_QK_DIM = DA_V_DIM // 2
RET_HEADS = 4
RET_V_DIM = D_MODEL // RET_HEADS
RET_K_DIM = RET_V_DIM // 2
D_FF = 4 * D_MODEL
N_MOD = 6
IN_COLS = 8 * D_MODEL

C_QA, C_KA, C_VA, C_QR, C_KR, C_VR, C_GR, C_GA, C_GB = (
    0, 1024, 2048, 3072, 3584, 4096, 5120, 6144, 7168)

LANES = 128
V7X_VMEM_LIMIT = 56 * 1024 * 1024

TM_PROJ = 512
TQ = 512
TK = 512
ATTN_HEADS_PER_STEP = 2
BOUND_MARGIN = 1.03
ROW_SUM_FLOOR = 2.0 ** -60
VT_ROWS = DA_V_DIM + 16
LOG2E = math.log2(math.e)
RET_CHUNK = 256
TM_MLP = 512

F32 = jnp.float32
BF16 = jnp.bfloat16
NT_DIMS = (((1,), (1,)), ((), ()))
TN_DIMS = (((0,), (0,)), ((), ()))


def _resident(shape):
    return pl.BlockSpec(shape, lambda *_: (0,) * len(shape),
                        pipeline_mode=pl.Buffered(1))


def _mod_kernel(c_ref, w_ref, b_ref, lq1_ref, lk1_ref, lq2_ref, lk2_ref,
                mod_ref, lam_ref, *, lambda_init):
    c = c_ref[...]
    sc = (c * jax.nn.sigmoid(c)).astype(BF16)
    mod_ref[...] = jnp.dot(sc, w_ref[...].astype(BF16),
                           preferred_element_type=F32) + b_ref[...]
    s1 = jnp.sum(lq1_ref[...] * lk1_ref[...], axis=-1, keepdims=True)
    s2 = jnp.sum(lq2_ref[...] * lk2_ref[...], axis=-1, keepdims=True)
    lam_ref[...] = jnp.exp(s1) - jnp.exp(s2) + lambda_init


def _modulation(c, w_ada, b_ada, lq1, lk1, lq2, lk2, lambda_init):
    b, d = c.shape
    n = w_ada.shape[1]
    bn = 1536
    vec = pl.BlockSpec((1, DA_QK_DIM), lambda j: (0, 0))
    return pl.pallas_call(
        functools.partial(_mod_kernel, lambda_init=lambda_init),
        grid=(n // bn,),
        in_specs=[pl.BlockSpec((b, d), lambda j: (0, 0)),
                  pl.BlockSpec((d, bn), lambda j: (0, j)),
                  pl.BlockSpec((1, bn), lambda j: (0, j)),
                  vec, vec, vec, vec],
        out_specs=[pl.BlockSpec((b, bn), lambda j: (0, j)),
                   pl.BlockSpec((1, 1), lambda j: (0, 0))],
        out_shape=[jax.ShapeDtypeStruct((b, n), F32),
                   jax.ShapeDtypeStruct((1, 1), F32)],
        compiler_params=pltpu.CompilerParams(
            dimension_semantics=("arbitrary",),
            vmem_limit_bytes=V7X_VMEM_LIMIT),
        name="modulation",
    )(c, w_ada, b_ada, lq1, lk1, lq2, lk2)


def _transpose_cast_kernel(w_ref, o_ref):
    o_ref[...] = w_ref[...].T.astype(BF16)


def _v_weight_transposed(w_in_l):
    d = w_in_l.shape[0]
    return pl.pallas_call(
        _transpose_cast_kernel,
        grid=(1,),
        in_specs=[pl.BlockSpec((d, d), lambda i: (0, C_VA // d))],
        out_specs=pl.BlockSpec((d, d), lambda i: (0, 0)),
        out_shape=jax.ShapeDtypeStruct((d, d), BF16),
        compiler_params=pltpu.CompilerParams(
            dimension_semantics=("arbitrary",),
            vmem_limit_bytes=V7X_VMEM_LIMIT),
        name="v_weight_t",
    )(w_in_l)


def _in_proj_kernel(x_ref, mod_ref, g1_ref, w_ref, wvt_ref, gq_ref, gk_ref,
                    cda_ref, sda_a_ref, sda_b_ref, cr_ref, sr_ref,
                    qa_ref, ka_ref, vt_ref, qkr_ref, vr_ref, ga_ref, gbs_ref):
    tm = x_ref.shape[0]
    x = x_ref[...]
    ms = jnp.mean(x * x, axis=-1, keepdims=True)
    h = x * lax.rsqrt(ms + EPS) * g1_ref[...]
    mod = mod_ref[0]
    hm = (h * (1.0 + mod[1:2, :]) + mod[0:1, :]).astype(BF16)

    def proj(col, width):
        return jnp.dot(hm, w_ref[:, col:col + width], preferred_element_type=F32)

    gi = lax.broadcasted_iota(jnp.int32, (256, 256), 0) // DA_QK_DIM
    gj = lax.broadcasted_iota(jnp.int32, (256, 256), 1) // DA_QK_DIM
    group_sum = jnp.where(gi == gj, 1.0, 0.0).astype(BF16)

    cda, sda_a, sda_b = cda_ref[...], sda_a_ref[...], sda_b_ref[...]

    def qk_norm_rope(acc, gain, out_ref):
        sq = (acc * acc).astype(BF16)
        for blk in range(D_MODEL // 256):
            ssq = jnp.dot(sq[:, blk * 256:(blk + 1) * 256], group_sum,
                          preferred_element_type=F32)
            rs = lax.rsqrt(ssq * (1.0 / DA_QK_DIM) + EPS)
            for half in range(2):
                c0 = blk * 256 + half * LANES
                a = acc[:, c0:c0 + LANES] * rs[:, half * LANES:(half + 1) * LANES] * gain
                rot = (a * cda + pltpu.roll(a, LANES - 32, 1) * sda_a
                       + pltpu.roll(a, 32, 1) * sda_b)
                out_ref[:, c0:c0 + LANES] = rot.astype(BF16)

    qk_norm_rope(proj(C_QA, D_MODEL), gq_ref[...], qa_ref)
    qk_norm_rope(proj(C_KA, D_MODEL), gk_ref[...], ka_ref)

    vt = lax.dot_general(wvt_ref[...], hm, NT_DIMS, preferred_element_type=F32)
    ones = jnp.ones((VT_ROWS - DA_V_DIM, tm), BF16)
    for hd in range(DA_HEADS):
        r0 = hd * VT_ROWS
        vt_ref[0, 0, r0:r0 + DA_V_DIM, :] = (
            vt[hd * DA_V_DIM:(hd + 1) * DA_V_DIM, :].astype(BF16))
        vt_ref[0, 0, r0 + DA_V_DIM:r0 + VT_ROWS, :] = ones

    qkr = proj(C_QR, D_MODEL)
    cr, sr = cr_ref[...], sr_ref[...]
    for j in range(D_MODEL // LANES):
        a = qkr[:, j * LANES:(j + 1) * LANES]
        rot = a * cr + pltpu.roll(a, LANES // 2, 1) * sr
        if j >= RET_HEADS:
            rot = rot * (RET_K_DIM ** -0.5)
        qkr_ref[:, j * LANES:(j + 1) * LANES] = rot.astype(BF16)

    vr_ref[...] = proj(C_VR, D_MODEL).astype(BF16)

    gr = proj(C_GR, D_MODEL)
    ga_ref[...] = jax.nn.sigmoid(proj(C_GA, D_MODEL)).astype(BF16)
    gbs_ref[...] = (jax.nn.sigmoid(proj(C_GB, D_MODEL))
                    * (gr * jax.nn.sigmoid(gr))).astype(BF16)


def _in_proj(x2, mod3, g1, w_in, w_vt, gq, gk, tabs, batch, seq):
    t, d = x2.shape
    tm = TM_PROJ
    tpb = seq // tm
    row = pl.BlockSpec((tm, d), lambda i: (i, 0))
    tab = pl.BlockSpec((tm, LANES), lambda i: (i % tpb, 0))
    vec = pl.BlockSpec((1, LANES), lambda i: (0, 0))
    out_row = jax.ShapeDtypeStruct((t, d), BF16)
    return pl.pallas_call(
        _in_proj_kernel,
        grid=(t // tm,),
        in_specs=[row,
                  pl.BlockSpec((1, N_MOD, d), lambda i: (i // tpb, 0, 0)),
                  pl.BlockSpec((1, d), lambda i: (0, 0)),
                  _resident((d, IN_COLS)),
                  _resident((d, d)),
                  vec, vec, tab, tab, tab, tab, tab],
        out_specs=[row, row,
                   pl.BlockSpec((1, 1, DA_HEADS * VT_ROWS, tm),
                                lambda i: (i // tpb, (i % tpb) // (TK // tm), 0,
                                           (i % tpb) % (TK // tm))),
                   row, row, row, row],
        out_shape=[out_row, out_row,
                   jax.ShapeDtypeStruct((batch, seq // TK, DA_HEADS * VT_ROWS, TK), BF16),
                   out_row, out_row, out_row, out_row],
        compiler_params=pltpu.CompilerParams(
            dimension_semantics=("arbitrary",),
            vmem_limit_bytes=V7X_VMEM_LIMIT),
        name="in_proj",
    )(x2, mod3, g1, w_in, w_vt, gq, gk, *tabs)


def _diff_attn_kernel(q_ref, k_ref, vt_ref, bias_ref, lam_ref, g_ref, o_ref,
                      sa_ref, sb_ref, acc_ref, m_ref, kmax_ref):
    qi = pl.program_id(2)
    heads = range(ATTN_HEADS_PER_STEP)
    lane = lax.broadcasted_iota(jnp.int32, (TQ, LANES), 1)
    qcat = []
    for hd in heads:
        q = q_ref[0, :, hd * LANES:(hd + 1) * LANES]
        zero = jnp.zeros_like(q)
        qcat.append(jnp.concatenate([jnp.where(lane < DA_QK_DIM, q, zero),
                                     jnp.where(lane >= DA_QK_DIM, q, zero)], axis=0))

    def k_block(j, hd):
        koff = pl.multiple_of(j * TK, TK)
        return k_ref[0, pl.ds(koff, TK), hd * LANES:(hd + 1) * LANES]

    def v_block(j, hd):
        return vt_ref[0, j, hd * VT_ROWS:(hd + 1) * VT_ROWS, :]

    @pl.when(qi == 0)
    def _():
        gi = lax.broadcasted_iota(jnp.int32, (LANES, LANES), 0) // DA_QK_DIM
        gj = lax.broadcasted_iota(jnp.int32, (LANES, LANES), 1) // DA_QK_DIM
        group_sum = jnp.where(gi == gj, 1.0, 0.0).astype(BF16)
        for hd in heads:
            k = k_ref[0, :, hd * LANES:(hd + 1) * LANES].astype(F32)
            norm2 = jnp.dot((k * k).astype(BF16), group_sum, preferred_element_type=F32)
            kmax_ref[hd] = jnp.max(norm2, axis=0, keepdims=True)

    bound = []
    for hd in heads:
        qsq = jnp.square(qcat[hd].astype(F32)).astype(BF16)
        kmax = jnp.broadcast_to(kmax_ref[hd], (16, LANES)).astype(BF16)
        ub2 = lax.dot_general(kmax, qsq, NT_DIMS, preferred_element_type=F32)
        bound.append(jnp.sqrt(ub2[0:1, :] * BOUND_MARGIN))

    def stream(j, diagonal):
        for hd in heads:
            s = lax.dot_general(k_block(j, hd), qcat[hd], NT_DIMS,
                                preferred_element_type=F32)
            if diagonal:
                s = s + bias_ref[...]
            e = jnp.exp2(s - bound[hd]).astype(BF16)
            pv = jnp.dot(v_block(j, hd), e, preferred_element_type=F32)
            acc_ref[hd] = pv if diagonal else acc_ref[hd] + pv

    stream(qi, True)

    def stream_body(j, carry):
        stream(j, False)
        return carry

    lax.fori_loop(0, qi, stream_body, 0)

    l_min = jnp.min(acc_ref[0, DA_V_DIM:DA_V_DIM + 1, :])
    for hd in heads[1:]:
        l_min = jnp.minimum(l_min, jnp.min(acc_ref[hd, DA_V_DIM:DA_V_DIM + 1, :]))

    def scores(j, s_ref):
        for hd in heads:
            s_ref[hd] = lax.dot_general(k_block(j, hd), qcat[hd], NT_DIMS,
                                        preferred_element_type=F32)

    def softmax_pv(j, s_ref, masked):
        for hd in heads:
            s = s_ref[hd]
            if masked:
                s = s + bias_ref[...]
            m_old = m_ref[hd]
            m_new = jnp.maximum(m_old, jnp.max(s, axis=0, keepdims=True))
            alpha = jnp.exp2(m_old - m_new)
            e = jnp.exp2(s - m_new).astype(BF16)
            pv = jnp.dot(v_block(j, hd), e, preferred_element_type=F32)
            acc_ref[hd] = alpha * acc_ref[hd] + pv
            m_ref[hd] = m_new

    @pl.when(jnp.logical_not(l_min >= ROW_SUM_FLOOR))
    def _():
        m_ref[...] = jnp.full(m_ref.shape, -jnp.inf, F32)
        acc_ref[...] = jnp.zeros(acc_ref.shape, F32)
        scores(0, sa_ref)

        def pair(t, carry):
            scores(2 * t + 1, sb_ref)
            softmax_pv(2 * t, sa_ref, False)
            scores(2 * t + 2, sa_ref)
            softmax_pv(2 * t + 1, sb_ref, False)
            return carry

        lax.fori_loop(0, qi // 2, pair, 0)

        @pl.when(qi % 2 == 0)
        def _():
            softmax_pv(qi, sa_ref, True)

        @pl.when(qi % 2 == 1)
        def _():
            scores(qi, sb_ref)
            softmax_pv(qi - 1, sa_ref, False)
            softmax_pv(qi, sb_ref, True)

    for hd in heads:
        acc = acc_ref[hd]
        inv_l = 1.0 / acc[DA_V_DIM:DA_V_DIM + 1, :]
        num = acc[:DA_V_DIM, :] * inv_l
        o = num[:, :TQ] - lam_ref[...] * num[:, TQ:]
        ms = jnp.mean(o * o, axis=0, keepdims=True)
        on = o * lax.rsqrt(ms + EPS) * g_ref[...]
        o_ref[0, :, hd * LANES:(hd + 1) * LANES] = on.T.astype(BF16)


def _diag_bias():
    kc = np.arange(TK)[:, None] // CHUNK
    qc = (np.arange(2 * TQ)[None, :] % TQ) // CHUNK
    return jnp.asarray(np.where(kc <= qc, 0.0, -np.inf).astype(np.float32))


def _diff_attn(qa, ka, vt, lam, g_col, batch, seq):
    hps = ATTN_HEADS_PER_STEP
    nq = seq // TQ
    return pl.pallas_call(
        _diff_attn_kernel,
        grid=(batch, DA_HEADS // hps, nq),
        in_specs=[pl.BlockSpec((1, TQ, hps * LANES), lambda b, h, i: (b, i, h)),
                  pl.BlockSpec((1, seq, hps * LANES), lambda b, h, i: (b, 0, h)),
                  pl.BlockSpec((1, seq // TK, hps * VT_ROWS, TK), lambda b, h, i: (b, 0, h, 0)),
                  _resident((TK, 2 * TQ)),
                  pl.BlockSpec((1, 1), lambda b, h, i: (0, 0)),
                  pl.BlockSpec((DA_V_DIM, 1), lambda b, h, i: (0, 0))],
        out_specs=pl.BlockSpec((1, TQ, hps * LANES), lambda b, h, i: (b, i, h)),
        out_shape=jax.ShapeDtypeStruct((batch, seq, D_MODEL), BF16),
        scratch_shapes=[pltpu.VMEM((hps, TK, 2 * TQ), F32),
                        pltpu.VMEM((hps, TK, 2 * TQ), F32),
                        pltpu.VMEM((hps, VT_ROWS, 2 * TQ), F32),
                        pltpu.VMEM((hps, 1, 2 * TQ), F32),
                        pltpu.VMEM((hps, 1, LANES), F32)],
        compiler_params=pltpu.CompilerParams(
            dimension_semantics=("arbitrary", "arbitrary", "arbitrary"),
            vmem_limit_bytes=V7X_VMEM_LIMIT),
        name="diff_attn",
    )(qa, ka, vt, _diag_bias(), lam, g_col)


def _retention_kernel(qk_ref, v_ref, g_ref, intra_ref, inner_ref, kvd_ref, o_ref,
                      state_ref, *, chunk_decay):
    @pl.when(pl.program_id(1) == 0)
    def _():
        state_ref[...] = jnp.zeros(state_ref.shape, F32)

    for hd in range(RET_HEADS):
        q = qk_ref[0, :, hd * RET_K_DIM:(hd + 1) * RET_K_DIM]
        k = qk_ref[0, :, (RET_HEADS + hd) * RET_K_DIM:(RET_HEADS + hd + 1) * RET_K_DIM]
        v = v_ref[0, :, hd * RET_V_DIM:(hd + 1) * RET_V_DIM]
        sc = lax.dot_general(q, k, NT_DIMS, preferred_element_type=F32) * intra_ref[hd]
        y = jnp.dot(sc.astype(BF16), v, preferred_element_type=F32)
        st = state_ref[hd]
        cross = jnp.dot(q, st.astype(BF16), preferred_element_type=F32)
        y = y + cross * inner_ref[hd]
        kd = (k.astype(F32) * kvd_ref[hd]).astype(BF16)
        upd = lax.dot_general(kd, v, TN_DIMS, preferred_element_type=F32)
        state_ref[hd] = st * chunk_decay[hd] + upd
        ms = jnp.mean(y * y, axis=-1, keepdims=True)
        o_ref[0, :, hd * RET_V_DIM:(hd + 1) * RET_V_DIM] = (
            y * lax.rsqrt(ms + EPS) * g_ref[...]).astype(BF16)


def _retention_tables(log_gamma, cr):
    n = np.arange(cr, dtype=np.float32)
    lg = log_gamma.astype(np.float32)[:, None, None]
    diff = n[:, None] - n[None, :]
    intra = np.where(diff >= 0, np.exp(np.maximum(diff, 0.0) * lg), 0.0).astype(np.float32)
    inner = np.exp((n + 1.0)[None, :, None] * lg).astype(np.float32)
    kvd = np.exp((cr - 1.0 - n)[None, :, None] * lg).astype(np.float32)
    inner = np.broadcast_to(inner, (RET_HEADS, cr, RET_V_DIM))
    kvd = np.broadcast_to(kvd, (RET_HEADS, cr, RET_K_DIM))
    chunk = tuple(float(np.exp(np.float32(cr) * g)) for g in log_gamma.astype(np.float32))
    return jnp.asarray(intra), jnp.asarray(inner), jnp.asarray(kvd), chunk


def _retention(qkr, vr, g_ret, log_gamma, batch, seq):
    cr = RET_CHUNK
    intra, inner, kvd, chunk_decay = _retention_tables(log_gamma, cr)
    rows = pl.BlockSpec((1, cr, D_MODEL), lambda b, c: (b, c, 0))
    return pl.pallas_call(
        functools.partial(_retention_kernel, chunk_decay=chunk_decay),
        grid=(batch, seq // cr),
        in_specs=[rows, rows,
                  pl.BlockSpec((1, RET_V_DIM), lambda b, c: (0, 0)),
                  _resident((RET_HEADS, cr, cr)),
                  _resident((RET_HEADS, cr, RET_V_DIM)),
                  _resident((RET_HEADS, cr, RET_K_DIM))],
        out_specs=rows,
        out_shape=jax.ShapeDtypeStruct((batch, seq, D_MODEL), BF16),
        scratch_shapes=[pltpu.VMEM((RET_HEADS, RET_K_DIM, RET_V_DIM), F32)],
        compiler_params=pltpu.CompilerParams(
            dimension_semantics=("arbitrary", "arbitrary"),
            vmem_limit_bytes=V7X_VMEM_LIMIT),
        name="retention",
    )(qkr, vr, g_ret, intra, inner, kvd)


def _out_mlp_kernel(x_ref, oa_ref, or_ref, ga_ref, gbs_ref, mod_ref, g2_ref,
                    wo_ref, wu_ref, wd_ref, out_ref):
    mod = mod_ref[0]
    u = (ga_ref[...].astype(F32) * oa_ref[...].astype(F32)
         + gbs_ref[...].astype(F32) * or_ref[...].astype(F32)).astype(BF16)
    y = jnp.dot(u, wo_ref[...], preferred_element_type=F32)
    x1 = x_ref[...] + mod[2:3, :] * y
    ms = jnp.mean(x1 * x1, axis=-1, keepdims=True)
    hn = x1 * lax.rsqrt(ms + EPS) * g2_ref[...]
    hff = (hn * (1.0 + mod[4:5, :]) + mod[3:4, :]).astype(BF16)
    up = jnp.dot(hff, wu_ref[...], preferred_element_type=F32)
    act = jnp.square(jnp.maximum(up, 0.0)).astype(BF16)
    ff = jnp.dot(act, wd_ref[...], preferred_element_type=F32)
    out_ref[...] = x1 + mod[5:6, :] * ff


def _out_mlp(x2, oa, orr, ga, gbs, mod3, g2, w_out, w_up, w_down, seq):
    t, d = x2.shape
    tm = TM_MLP
    tpb = seq // tm
    row = pl.BlockSpec((tm, d), lambda i: (i, 0))
    return pl.pallas_call(
        _out_mlp_kernel,
        grid=(t // tm,),
        in_specs=[row, row, row, row, row,
                  pl.BlockSpec((1, N_MOD, d), lambda i: (i // tpb, 0, 0)),
                  pl.BlockSpec((1, d), lambda i: (0, 0)),
                  _resident((d, d)), _resident((d, D_FF)), _resident((D_FF, d))],
        out_specs=row,
        out_shape=jax.ShapeDtypeStruct((t, d), F32),
        compiler_params=pltpu.CompilerParams(
            dimension_semantics=("arbitrary",),
            vmem_limit_bytes=V7X_VMEM_LIMIT),
        name="out_mlp",
    )(x2, oa, orr, ga, gbs, mod3, g2, w_out, w_up, w_down)


def _rope_tables(seq):
    pos = jnp.arange(seq, dtype=F32)[:, None]
    da_inv = ROPE_THETA ** (-jnp.arange(0, DA_QK_DIM, 2, dtype=F32) / DA_QK_DIM)
    ang = pos * da_inv[None, :]
    cos, sin = jnp.cos(ang), jnp.sin(ang)
    zero = jnp.zeros_like(sin)
    cda = jnp.tile(cos, (1, 4))
    sda_a = jnp.tile(jnp.concatenate([-sin, zero], axis=1), (1, 2))
    sda_b = jnp.tile(jnp.concatenate([zero, sin], axis=1), (1, 2))
    ret_inv = 1.0 / (ROPE_THETA ** jnp.linspace(0.0, 1.0, RET_K_DIM // 2, dtype=F32))
    ang_r = pos * ret_inv[None, :]
    cos_r, sin_r = jnp.cos(ang_r), jnp.sin(ang_r)
    cr = jnp.concatenate([cos_r, cos_r], axis=1)
    sr = jnp.concatenate([-sin_r, sin_r], axis=1)
    return cda, sda_a, sda_b, cr, sr


def kernel(x, c, w_ada, b_ada, g_norm1, w_in, g_q, g_k, lambda_q1, lambda_k1,
           lambda_q2, lambda_k2, g_da_out, g_ret_out, w_out, g_norm2, w_up, w_down):
    batch, seq, d = x.shape
    depth = w_ada.shape[0]
    log_gamma = np.log(1.0 - 2.0 ** (-5.0 - np.arange(RET_HEADS))).astype(np.float32)
    tabs = _rope_tables(seq)

    for l in range(depth):
        lambda_init = 0.8 - 0.6 * math.exp(-0.3 * l)
        mod, lam = _modulation(c, w_ada[l], b_ada[l][None, :],
                               lambda_q1[l][None, :], lambda_k1[l][None, :],
                               lambda_q2[l][None, :], lambda_k2[l][None, :],
                               lambda_init)
        mod3 = mod.reshape(batch, N_MOD, d)
        x2 = x.reshape(batch * seq, d)

        w_in_b = w_in[l].astype(BF16)
        w_vt = _v_weight_transposed(w_in[l])
        gq = (jnp.tile(g_q[l], 2) * (DA_QK_DIM ** -0.5 * LOG2E))[None, :]
        gk = jnp.tile(g_k[l], 2)[None, :]
        qa, ka, vt, qkr, vr, ga, gbs = _in_proj(
            x2, mod3, g_norm1[l][None, :], w_in_b, w_vt, gq, gk, tabs, batch, seq)

        g_col = (g_da_out[l] * (1.0 - lambda_init))[:, None]
        oa = _diff_attn(qa.reshape(batch, seq, d), ka.reshape(batch, seq, d), vt,
                        lam, g_col, batch, seq)
        orr = _retention(qkr.reshape(batch, seq, d), vr.reshape(batch, seq, d),
                         g_ret_out[l][None, :], log_gamma, batch, seq)

        out = _out_mlp(x2, oa.reshape(batch * seq, d), orr.reshape(batch * seq, d),
                       ga, gbs, mod3, g_norm2[l][None, :],
                       w_out[l].astype(BF16), w_up[l].astype(BF16),
                       w_down[l].astype(BF16), seq)
        x = out.reshape(batch, seq, d)
    return x
```

```python
import functools
import math

import numpy as np
import jax
import jax.numpy as jnp
from jax import lax
from jax.experimental import pallas as pl
from jax.experimental.pallas import tpu as pltpu

D_MODEL = 1024
CHUNK = 64
ROPE_THETA = 10000.0
EPS = 1e-6
DA_HEADS = 8
DA_V_DIM = D_MODEL // DA_HEADS
DA_QK_DIM = DA_V_DIM // 2
RET_HEADS = 4
RET_V_DIM = D_MODEL // RET_HEADS
RET_K_DIM = RET_V_DIM // 2
D_FF = 4 * D_MODEL
N_MOD = 6
IN_COLS = 8 * D_MODEL

C_QA, C_KA, C_VA, C_QR, C_KR, C_VR, C_GR, C_GA, C_GB = (
    0, 1024, 2048, 3072, 3584, 4096, 5120, 6144, 7168)

LANES = 128
V7X_VMEM_LIMIT = 56 * 1024 * 1024

TM_PROJ = 512
TQ = 512
TK = 512
ATTN_HEADS_PER_STEP = 2
BOUND_MARGIN = 1.03
ROW_SUM_FLOOR = 2.0 ** -60
LOG2E = math.log2(math.e)
RET_CHUNK = 256
RET_CHUNKS_PER_STEP = 4
TM_MLP = 512

F32 = jnp.float32
BF16 = jnp.bfloat16
NT_DIMS = (((1,), (1,)), ((), ()))
TN_DIMS = (((0,), (0,)), ((), ()))


def _resident(shape):
    return pl.BlockSpec(shape, lambda *_: (0,) * len(shape),
                        pipeline_mode=pl.Buffered(1))


def _mod_kernel(c_ref, w_ref, b_ref, lq1_ref, lk1_ref, lq2_ref, lk2_ref,
                mod_ref, lam_ref, *, lambda_init):
    c = c_ref[...]
    sc = (c * jax.nn.sigmoid(c)).astype(BF16)
    mod_ref[...] = jnp.dot(sc, w_ref[...].astype(BF16),
                           preferred_element_type=F32) + b_ref[...]
    s1 = jnp.sum(lq1_ref[...] * lk1_ref[...], axis=-1, keepdims=True)
    s2 = jnp.sum(lq2_ref[...] * lk2_ref[...], axis=-1, keepdims=True)
    lam_ref[...] = jnp.exp(s1) - jnp.exp(s2) + lambda_init


def _modulation(c, w_ada, b_ada, lq1, lk1, lq2, lk2, lambda_init):
    b, d = c.shape
    n = w_ada.shape[1]
    bn = 1536
    vec = pl.BlockSpec((1, DA_QK_DIM), lambda j: (0, 0))
    return pl.pallas_call(
        functools.partial(_mod_kernel, lambda_init=lambda_init),
        grid=(n // bn,),
        in_specs=[pl.BlockSpec((b, d), lambda j: (0, 0)),
                  pl.BlockSpec((d, bn), lambda j: (0, j)),
                  pl.BlockSpec((1, bn), lambda j: (0, j)),
                  vec, vec, vec, vec],
        out_specs=[pl.BlockSpec((b, bn), lambda j: (0, j)),
                   pl.BlockSpec((1, 1), lambda j: (0, 0))],
        out_shape=[jax.ShapeDtypeStruct((b, n), F32),
                   jax.ShapeDtypeStruct((1, 1), F32)],
        compiler_params=pltpu.CompilerParams(
            dimension_semantics=("arbitrary",),
            vmem_limit_bytes=V7X_VMEM_LIMIT),
        name="modulation",
    )(c, w_ada, b_ada, lq1, lk1, lq2, lk2)


def _transpose_cast_kernel(w_ref, o_ref):
    o_ref[...] = w_ref[...].T.astype(BF16)


def _v_weight_transposed(w_in_l):
    d = w_in_l.shape[0]
    return pl.pallas_call(
        _transpose_cast_kernel,
        grid=(1,),
        in_specs=[pl.BlockSpec((d, d), lambda i: (0, C_VA // d))],
        out_specs=pl.BlockSpec((d, d), lambda i: (0, 0)),
        out_shape=jax.ShapeDtypeStruct((d, d), BF16),
        compiler_params=pltpu.CompilerParams(
            dimension_semantics=("arbitrary",),
            vmem_limit_bytes=V7X_VMEM_LIMIT),
        name="v_weight_t",
    )(w_in_l)


def _in_proj_kernel(x_ref, mod_ref, g1_ref, w_ref, wvt_ref, gq_ref, gk_ref,
                    cda_ref, sda_a_ref, sda_b_ref, cr_ref, sr_ref,
                    qa_ref, ka_ref, vt_ref, qkr_ref, vr_ref, ga_ref, gbs_ref):
    tm = x_ref.shape[0]
    x = x_ref[...]
    ms = jnp.mean(x * x, axis=-1, keepdims=True)
    h = x * lax.rsqrt(ms + EPS) * g1_ref[...]
    mod = mod_ref[0]
    hm = (h * (1.0 + mod[1:2, :]) + mod[0:1, :]).astype(BF16)

    def proj(col, width):
        return jnp.dot(hm, w_ref[:, col:col + width], preferred_element_type=F32)

    gi = lax.broadcasted_iota(jnp.int32, (256, 256), 0) // DA_QK_DIM
    gj = lax.broadcasted_iota(jnp.int32, (256, 256), 1) // DA_QK_DIM
    group_sum = jnp.where(gi == gj, 1.0, 0.0).astype(BF16)

    cda, sda_a, sda_b = cda_ref[...], sda_a_ref[...], sda_b_ref[...]

    def qk_norm_rope(acc, gain, out_ref):
        sq = (acc * acc).astype(BF16)
        for blk in range(D_MODEL // 256):
            ssq = jnp.dot(sq[:, blk * 256:(blk + 1) * 256], group_sum,
                          preferred_element_type=F32)
            rs = lax.rsqrt(ssq * (1.0 / DA_QK_DIM) + EPS)
            for half in range(2):
                c0 = blk * 256 + half * LANES
                a = acc[:, c0:c0 + LANES] * rs[:, half * LANES:(half + 1) * LANES] * gain
                rot = (a * cda + pltpu.roll(a, LANES - 32, 1) * sda_a
                       + pltpu.roll(a, 32, 1) * sda_b)
                out_ref[:, c0:c0 + LANES] = rot.astype(BF16)

    qk_norm_rope(proj(C_QA, D_MODEL), gq_ref[...], qa_ref)
    qk_norm_rope(proj(C_KA, D_MODEL), gk_ref[...], ka_ref)

    vt = lax.dot_general(wvt_ref[...], hm, NT_DIMS, preferred_element_type=F32)
    vt_ref[0, 0] = vt.astype(BF16)

    qkr = proj(C_QR, D_MODEL)
    cr, sr = cr_ref[...], sr_ref[...]
    for j in range(D_MODEL // LANES):
        a = qkr[:, j * LANES:(j + 1) * LANES]
        rot = a * cr + pltpu.roll(a, LANES // 2, 1) * sr
        if j >= RET_HEADS:
            rot = rot * (RET_K_DIM ** -0.5)
        qkr_ref[:, j * LANES:(j + 1) * LANES] = rot.astype(BF16)

    vr_ref[...] = proj(C_VR, D_MODEL).astype(BF16)

    gr = proj(C_GR, D_MODEL)
    ga_ref[...] = jax.nn.sigmoid(proj(C_GA, D_MODEL)).astype(BF16)
    gbs_ref[...] = (jax.nn.sigmoid(proj(C_GB, D_MODEL))
                    * (gr * jax.nn.sigmoid(gr))).astype(BF16)


def _in_proj(x2, mod3, g1, w_in, w_vt, gq, gk, tabs, batch, seq):
    t, d = x2.shape
    tm = TM_PROJ
    tpb = seq // tm
    row = pl.BlockSpec((tm, d), lambda i: (i, 0))
    tab = pl.BlockSpec((tm, LANES), lambda i: (i % tpb, 0))
    vec = pl.BlockSpec((1, LANES), lambda i: (0, 0))
    out_row = jax.ShapeDtypeStruct((t, d), BF16)
    return pl.pallas_call(
        _in_proj_kernel,
        grid=(t // tm,),
        in_specs=[row,
                  pl.BlockSpec((1, N_MOD, d), lambda i: (i // tpb, 0, 0)),
                  pl.BlockSpec((1, d), lambda i: (0, 0)),
                  _resident((d, IN_COLS)),
                  _resident((d, d)),
                  vec, vec, tab, tab, tab, tab, tab],
        out_specs=[row, row,
                   pl.BlockSpec((1, 1, d, tm),
                                lambda i: (i // tpb, (i % tpb) // (TK // tm), 0,
                                           (i % tpb) % (TK // tm))),
                   row, row, row, row],
        out_shape=[out_row, out_row,
                   jax.ShapeDtypeStruct((batch, seq // TK, d, TK), BF16),
                   out_row, out_row, out_row, out_row],
        compiler_params=pltpu.CompilerParams(
            dimension_semantics=("arbitrary",),
            vmem_limit_bytes=V7X_VMEM_LIMIT),
        name="in_proj",
    )(x2, mod3, g1, w_in, w_vt, gq, gk, *tabs)


def _diff_attn_kernel(q_ref, k_ref, vt_ref, bias_ref, lam_ref, g_ref, o_ref,
                      sa_ref, sb_ref, acc_ref, l_ref, m_ref, kmax_ref):
    qi = pl.program_id(2)
    heads = range(ATTN_HEADS_PER_STEP)
    lane = lax.broadcasted_iota(jnp.int32, (TQ, LANES), 1)
    qcat = []
    for hd in heads:
        q = q_ref[0, :, hd * LANES:(hd + 1) * LANES]
        zero = jnp.zeros_like(q)
        qcat.append(jnp.concatenate([jnp.where(lane < DA_QK_DIM, q, zero),
                                     jnp.where(lane >= DA_QK_DIM, q, zero)], axis=0))

    def k_block(j, hd):
        koff = pl.multiple_of(j * TK, TK)
        return k_ref[0, pl.ds(koff, TK), hd * LANES:(hd + 1) * LANES]

    def v_block(j, hd):
        return vt_ref[0, j, hd * DA_V_DIM:(hd + 1) * DA_V_DIM, :]

    @pl.when(qi == 0)
    def _():
        gi = lax.broadcasted_iota(jnp.int32, (LANES, LANES), 0) // DA_QK_DIM
        gj = lax.broadcasted_iota(jnp.int32, (LANES, LANES), 1) // DA_QK_DIM
        group_sum = jnp.where(gi == gj, 1.0, 0.0).astype(BF16)
        for hd in heads:
            k = k_ref[0, :, hd * LANES:(hd + 1) * LANES].astype(F32)
            norm2 = jnp.dot((k * k).astype(BF16), group_sum, preferred_element_type=F32)
            kmax_ref[hd] = jnp.max(norm2, axis=0, keepdims=True)

    bound = []
    for hd in heads:
        qsq = jnp.square(qcat[hd].astype(F32)).astype(BF16)
        kmax = jnp.broadcast_to(kmax_ref[hd], (16, LANES)).astype(BF16)
        ub2 = lax.dot_general(kmax, qsq, NT_DIMS, preferred_element_type=F32)
        bound.append(jnp.sqrt(ub2[0:1, :] * BOUND_MARGIN))

    def stream(blocks, diagonal):
        for hd in heads:
            pv, row_sum = None, None
            for j in blocks:
                s = lax.dot_general(k_block(j, hd), qcat[hd], NT_DIMS,
                                    preferred_element_type=F32)
                if diagonal:
                    s = s + bias_ref[...]
                e = jnp.exp2(s - bound[hd])
                part_sum = jnp.sum(e, axis=0, keepdims=True)
                part = jnp.dot(v_block(j, hd), e.astype(BF16),
                               preferred_element_type=F32)
                pv = part if pv is None else pv + part
                row_sum = part_sum if row_sum is None else row_sum + part_sum
            acc_ref[hd] = pv if diagonal else acc_ref[hd] + pv
            l_ref[hd] = row_sum if diagonal else l_ref[hd] + row_sum

    stream([qi], True)

    def stream_body(t, carry):
        stream([2 * t, 2 * t + 1], False)
        return carry

    lax.fori_loop(0, qi // 2, stream_body, 0)

    @pl.when(qi % 2 == 1)
    def _():
        stream([qi - 1], False)

    def finalize():
        for hd in heads:
            num = acc_ref[hd] * (1.0 / l_ref[hd])
            o = num[:, :TQ] - lam_ref[...] * num[:, TQ:]
            ms = jnp.mean(o * o, axis=0, keepdims=True)
            on = o * lax.rsqrt(ms + EPS) * g_ref[...]
            o_ref[0, :, hd * LANES:(hd + 1) * LANES] = on.T.astype(BF16)

    finalize()
    l_min = jnp.min(l_ref[...])

    def scores(j, s_ref):
        for hd in heads:
            s_ref[hd] = lax.dot_general(k_block(j, hd), qcat[hd], NT_DIMS,
                                        preferred_element_type=F32)

    def softmax_pv(j, s_ref, masked):
        for hd in heads:
            s = s_ref[hd]
            if masked:
                s = s + bias_ref[...]
            m_old = m_ref[hd]
            m_new = jnp.maximum(m_old, jnp.max(s, axis=0, keepdims=True))
            alpha = jnp.exp2(m_old - m_new)
            e = jnp.exp2(s - m_new)
            pv = jnp.dot(v_block(j, hd), e.astype(BF16), preferred_element_type=F32)
            acc_ref[hd] = alpha * acc_ref[hd] + pv
            l_ref[hd] = alpha * l_ref[hd] + jnp.sum(e, axis=0, keepdims=True)
            m_ref[hd] = m_new

    @pl.when(jnp.logical_not(l_min >= ROW_SUM_FLOOR))
    def _():
        m_ref[...] = jnp.full(m_ref.shape, -jnp.inf, F32)
        acc_ref[...] = jnp.zeros(acc_ref.shape, F32)
        l_ref[...] = jnp.zeros(l_ref.shape, F32)
        scores(0, sa_ref)

        def pair(t, carry):
            scores(2 * t + 1, sb_ref)
            softmax_pv(2 * t, sa_ref, False)
            scores(2 * t + 2, sa_ref)
            softmax_pv(2 * t + 1, sb_ref, False)
            return carry

        lax.fori_loop(0, qi // 2, pair, 0)

        @pl.when(qi % 2 == 0)
        def _():
            softmax_pv(qi, sa_ref, True)

        @pl.when(qi % 2 == 1)
        def _():
            scores(qi, sb_ref)
            softmax_pv(qi - 1, sa_ref, False)
            softmax_pv(qi, sb_ref, True)

        finalize()


def _diag_bias():
    kc = np.arange(TK)[:, None] // CHUNK
    qc = (np.arange(2 * TQ)[None, :] % TQ) // CHUNK
    return jnp.asarray(np.where(kc <= qc, 0.0, -np.inf).astype(np.float32))


def _diff_attn(qa, ka, vt, lam, g_col, batch, seq):
    hps = ATTN_HEADS_PER_STEP
    nq = seq // TQ
    return pl.pallas_call(
        _diff_attn_kernel,
        grid=(batch, DA_HEADS // hps, nq),
        in_specs=[pl.BlockSpec((1, TQ, hps * LANES), lambda b, h, i: (b, i, h)),
                  pl.BlockSpec((1, seq, hps * LANES), lambda b, h, i: (b, 0, h)),
                  pl.BlockSpec((1, seq // TK, hps * DA_V_DIM, TK), lambda b, h, i: (b, 0, h, 0)),
                  _resident((TK, 2 * TQ)),
                  pl.BlockSpec((1, 1), lambda b, h, i: (0, 0)),
                  pl.BlockSpec((DA_V_DIM, 1), lambda b, h, i: (0, 0))],
        out_specs=pl.BlockSpec((1, TQ, hps * LANES), lambda b, h, i: (b, i, h)),
        out_shape=jax.ShapeDtypeStruct((batch, seq, D_MODEL), BF16),
        scratch_shapes=[pltpu.VMEM((hps, TK, 2 * TQ), F32),
                        pltpu.VMEM((hps, TK, 2 * TQ), F32),
                        pltpu.VMEM((hps, DA_V_DIM, 2 * TQ), F32),
                        pltpu.VMEM((hps, 1, 2 * TQ), F32),
                        pltpu.VMEM((hps, 1, 2 * TQ), F32),
                        pltpu.VMEM((hps, 1, LANES), F32)],
        compiler_params=pltpu.CompilerParams(
            dimension_semantics=("arbitrary", "arbitrary", "arbitrary"),
            vmem_limit_bytes=V7X_VMEM_LIMIT),
        name="diff_attn",
    )(qa, ka, vt, _diag_bias(), lam, g_col)


def _retention_kernel(qk_ref, v_ref, g_ref, intra_ref, inner_ref, kvd_ref, o_ref,
                      state_ref, *, chunk_decay):
    @pl.when(pl.program_id(1) == 0)
    def _():
        state_ref[...] = jnp.zeros(state_ref.shape, F32)

    cr = intra_ref.shape[1]
    for ci in range(RET_CHUNKS_PER_STEP):
        rows = slice(ci * cr, (ci + 1) * cr)
        for hd in range(RET_HEADS):
            q = qk_ref[0, rows, hd * RET_K_DIM:(hd + 1) * RET_K_DIM]
            k = qk_ref[0, rows, (RET_HEADS + hd) * RET_K_DIM:(RET_HEADS + hd + 1) * RET_K_DIM]
            v = v_ref[0, rows, hd * RET_V_DIM:(hd + 1) * RET_V_DIM]
            sc = lax.dot_general(q, k, NT_DIMS, preferred_element_type=F32) * intra_ref[hd]
            y = jnp.dot(sc.astype(BF16), v, preferred_element_type=F32)
            st = state_ref[hd]
            cross = jnp.dot(q, st.astype(BF16), preferred_element_type=F32)
            y = y + cross * inner_ref[hd]
            kd = (k.astype(F32) * kvd_ref[hd]).astype(BF16)
            upd = lax.dot_general(kd, v, TN_DIMS, preferred_element_type=F32)
            state_ref[hd] = st * chunk_decay[hd] + upd
            ms = jnp.mean(y * y, axis=-1, keepdims=True)
            o_ref[0, rows, hd * RET_V_DIM:(hd + 1) * RET_V_DIM] = (
                y * lax.rsqrt(ms + EPS) * g_ref[...]).astype(BF16)


def _retention_tables(log_gamma, cr):
    n = np.arange(cr, dtype=np.float32)
    lg = log_gamma.astype(np.float32)[:, None, None]
    diff = n[:, None] - n[None, :]
    intra = np.where(diff >= 0, np.exp(np.maximum(diff, 0.0) * lg), 0.0).astype(np.float32)
    inner = np.exp((n + 1.0)[None, :, None] * lg).astype(np.float32)
    kvd = np.exp((cr - 1.0 - n)[None, :, None] * lg).astype(np.float32)
    inner = np.broadcast_to(inner, (RET_HEADS, cr, RET_V_DIM))
    kvd = np.broadcast_to(kvd, (RET_HEADS, cr, RET_K_DIM))
    chunk = tuple(float(np.exp(np.float32(cr) * g)) for g in log_gamma.astype(np.float32))
    return jnp.asarray(intra), jnp.asarray(inner), jnp.asarray(kvd), chunk


def _retention(qkr, vr, g_ret, log_gamma, batch, seq):
    cr = RET_CHUNK
    intra, inner, kvd, chunk_decay = _retention_tables(log_gamma, cr)
    step_rows = cr * RET_CHUNKS_PER_STEP
    rows = pl.BlockSpec((1, step_rows, D_MODEL), lambda b, c: (b, c, 0))
    return pl.pallas_call(
        functools.partial(_retention_kernel, chunk_decay=chunk_decay),
        grid=(batch, seq // step_rows),
        in_specs=[rows, rows,
                  pl.BlockSpec((1, RET_V_DIM), lambda b, c: (0, 0)),
                  _resident((RET_HEADS, cr, cr)),
                  _resident((RET_HEADS, cr, RET_V_DIM)),
                  _resident((RET_HEADS, cr, RET_K_DIM))],
        out_specs=rows,
        out_shape=jax.ShapeDtypeStruct((batch, seq, D_MODEL), BF16),
        scratch_shapes=[pltpu.VMEM((RET_HEADS, RET_K_DIM, RET_V_DIM), F32)],
        compiler_params=pltpu.CompilerParams(
            dimension_semantics=("arbitrary", "arbitrary"),
            vmem_limit_bytes=V7X_VMEM_LIMIT),
        name="retention",
    )(qkr, vr, g_ret, intra, inner, kvd)


def _out_mlp_kernel(x_ref, oa_ref, or_ref, ga_ref, gbs_ref, mod_ref, g2_ref,
                    wo_ref, wu_ref, wd_ref, out_ref):
    mod = mod_ref[0]
    u = (ga_ref[...].astype(F32) * oa_ref[...].astype(F32)
         + gbs_ref[...].astype(F32) * or_ref[...].astype(F32)).astype(BF16)
    y = jnp.dot(u, wo_ref[...], preferred_element_type=F32)
    x1 = x_ref[...] + mod[2:3, :] * y
    ms = jnp.mean(x1 * x1, axis=-1, keepdims=True)
    hn = x1 * lax.rsqrt(ms + EPS) * g2_ref[...]
    hff = (hn * (1.0 + mod[4:5, :]) + mod[3:4, :]).astype(BF16)
    up = jnp.dot(hff, wu_ref[...], preferred_element_type=F32)
    act = jnp.square(jnp.maximum(up, 0.0)).astype(BF16)
    ff = jnp.dot(act, wd_ref[...], preferred_element_type=F32)
    out_ref[...] = x1 + mod[5:6, :] * ff


def _out_mlp(x2, oa, orr, ga, gbs, mod3, g2, w_out, w_up, w_down, seq):
    t, d = x2.shape
    tm = TM_MLP
    tpb = seq // tm
    row = pl.BlockSpec((tm, d), lambda i: (i, 0))
    return pl.pallas_call(
        _out_mlp_kernel,
        grid=(t // tm,),
        in_specs=[row, row, row, row, row,
                  pl.BlockSpec((1, N_MOD, d), lambda i: (i // tpb, 0, 0)),
                  pl.BlockSpec((1, d), lambda i: (0, 0)),
                  _resident((d, d)), _resident((d, D_FF)), _resident((D_FF, d))],
        out_specs=row,
        out_shape=jax.ShapeDtypeStruct((t, d), F32),
        compiler_params=pltpu.CompilerParams(
            dimension_semantics=("arbitrary",),
            vmem_limit_bytes=V7X_VMEM_LIMIT),
        name="out_mlp",
    )(x2, oa, orr, ga, gbs, mod3, g2, w_out, w_up, w_down)


def _rope_tables(seq):
    pos = jnp.arange(seq, dtype=F32)[:, None]
    da_inv = ROPE_THETA ** (-jnp.arange(0, DA_QK_DIM, 2, dtype=F32) / DA_QK_DIM)
    ang = pos * da_inv[None, :]
    cos, sin = jnp.cos(ang), jnp.sin(ang)
    zero = jnp.zeros_like(sin)
    cda = jnp.tile(cos, (1, 4))
    sda_a = jnp.tile(jnp.concatenate([-sin, zero], axis=1), (1, 2))
    sda_b = jnp.tile(jnp.concatenate([zero, sin], axis=1), (1, 2))
    ret_inv = 1.0 / (ROPE_THETA ** jnp.linspace(0.0, 1.0, RET_K_DIM // 2, dtype=F32))
    ang_r = pos * ret_inv[None, :]
    cos_r, sin_r = jnp.cos(ang_r), jnp.sin(ang_r)
    cr = jnp.concatenate([cos_r, cos_r], axis=1)
    sr = jnp.concatenate([-sin_r, sin_r], axis=1)
    return cda, sda_a, sda_b, cr, sr


def kernel(x, c, w_ada, b_ada, g_norm1, w_in, g_q, g_k, lambda_q1, lambda_k1,
           lambda_q2, lambda_k2, g_da_out, g_ret_out, w_out, g_norm2, w_up, w_down):
    batch, seq, d = x.shape
    depth = w_ada.shape[0]
    log_gamma = np.log(1.0 - 2.0 ** (-5.0 - np.arange(RET_HEADS))).astype(np.float32)
    tabs = _rope_tables(seq)

    for l in range(depth):
        lambda_init = 0.8 - 0.6 * math.exp(-0.3 * l)
        mod, lam = _modulation(c, w_ada[l], b_ada[l][None, :],
                               lambda_q1[l][None, :], lambda_k1[l][None, :],
                               lambda_q2[l][None, :], lambda_k2[l][None, :],
                               lambda_init)
        mod3 = mod.reshape(batch, N_MOD, d)
        x2 = x.reshape(batch * seq, d)

        w_in_b = w_in[l].astype(BF16)
        w_vt = _v_weight_transposed(w_in[l])
        gq = (jnp.tile(g_q[l], 2) * (DA_QK_DIM ** -0.5 * LOG2E))[None, :]
        gk = jnp.tile(g_k[l], 2)[None, :]
        qa, ka, vt, qkr, vr, ga, gbs = _in_proj(
            x2, mod3, g_norm1[l][None, :], w_in_b, w_vt, gq, gk, tabs, batch, seq)

        g_col = (g_da_out[l] * (1.0 - lambda_init))[:, None]
        oa = _diff_attn(qa.reshape(batch, seq, d), ka.reshape(batch, seq, d), vt,
                        lam, g_col, batch, seq)
        orr = _retention(qkr.reshape(batch, seq, d), vr.reshape(batch, seq, d),
                         g_ret_out[l][None, :], log_gamma, batch, seq)

        out = _out_mlp(x2, oa.reshape(batch * seq, d), orr.reshape(batch * seq, d),
                       ga, gbs, mod3, g_norm2[l][None, :],
                       w_out[l].astype(BF16), w_up[l].astype(BF16),
                       w_down[l].astype(BF16), seq)
        x = out.reshape(batch, seq, d)
    return x
```

```python
import functools
import math

import numpy as np
import jax
import jax.numpy as jnp
from jax import lax
from jax.experimental import pallas as pl
from jax.experimental.pallas import tpu as pltpu

D_MODEL = 1024
CHUNK = 64
ROPE_THETA = 10000.0
EPS = 1e-6
DA_HEADS = 8
DA_V_DIM = D_MODEL // DA_HEADS
DA_QK_DIM = DA_V_DIM // 2
RET_HEADS = 4
RET_V_DIM = D_MODEL // RET_HEADS
RET_K_DIM = RET_V_DIM // 2
D_FF = 4 * D_MODEL
N_MOD = 6
IN_COLS = 8 * D_MODEL

C_QA, C_KA, C_VA, C_QR, C_KR, C_VR, C_GR, C_GA, C_GB = (
    0, 1024, 2048, 3072, 3584, 4096, 5120, 6144, 7168)

LANES = 128
V7X_VMEM_LIMIT = 56 * 1024 * 1024

TM_PROJ = 512
TQ = 512
TK = 512
ATTN_HEADS_PER_STEP = 4
BOUND_MARGIN = 1.03
ROW_SUM_FLOOR = 2.0 ** -60
LOG2E = math.log2(math.e)
RET_CHUNK = 256
RET_CHUNKS_PER_STEP = 4
TM_MLP = 512

F32 = jnp.float32
BF16 = jnp.bfloat16
NT_DIMS = (((1,), (1,)), ((), ()))
TN_DIMS = (((0,), (0,)), ((), ()))


def _resident(shape):
    return pl.BlockSpec(shape, lambda *_: (0,) * len(shape),
                        pipeline_mode=pl.Buffered(1))


def _mod_kernel(c_ref, w_ref, b_ref, lq1_ref, lk1_ref, lq2_ref, lk2_ref,
                mod_ref, lam_ref, *, lambda_init):
    c = c_ref[...]
    sc = (c * jax.nn.sigmoid(c)).astype(BF16)
    mod_ref[...] = jnp.dot(sc, w_ref[...].astype(BF16),
                           preferred_element_type=F32) + b_ref[...]
    s1 = jnp.sum(lq1_ref[...] * lk1_ref[...], axis=-1, keepdims=True)
    s2 = jnp.sum(lq2_ref[...] * lk2_ref[...], axis=-1, keepdims=True)
    lam_ref[...] = jnp.exp(s1) - jnp.exp(s2) + lambda_init


def _modulation(c, w_ada, b_ada, lq1, lk1, lq2, lk2, lambda_init):
    b, d = c.shape
    n = w_ada.shape[1]
    bn = 1536
    vec = pl.BlockSpec((1, DA_QK_DIM), lambda j: (0, 0))
    return pl.pallas_call(
        functools.partial(_mod_kernel, lambda_init=lambda_init),
        grid=(n // bn,),
        in_specs=[pl.BlockSpec((b, d), lambda j: (0, 0)),
                  pl.BlockSpec((d, bn), lambda j: (0, j)),
                  pl.BlockSpec((1, bn), lambda j: (0, j)),
                  vec, vec, vec, vec],
        out_specs=[pl.BlockSpec((b, bn), lambda j: (0, j)),
                   pl.BlockSpec((1, 1), lambda j: (0, 0))],
        out_shape=[jax.ShapeDtypeStruct((b, n), F32),
                   jax.ShapeDtypeStruct((1, 1), F32)],
        compiler_params=pltpu.CompilerParams(
            dimension_semantics=("arbitrary",),
            vmem_limit_bytes=V7X_VMEM_LIMIT),
        name="modulation",
    )(c, w_ada, b_ada, lq1, lk1, lq2, lk2)


def _transpose_cast_kernel(w_ref, o_ref):
    o_ref[...] = w_ref[...].T.astype(BF16)


def _v_weight_transposed(w_in_l):
    d = w_in_l.shape[0]
    return pl.pallas_call(
        _transpose_cast_kernel,
        grid=(1,),
        in_specs=[pl.BlockSpec((d, d), lambda i: (0, C_VA // d))],
        out_specs=pl.BlockSpec((d, d), lambda i: (0, 0)),
        out_shape=jax.ShapeDtypeStruct((d, d), BF16),
        compiler_params=pltpu.CompilerParams(
            dimension_semantics=("arbitrary",),
            vmem_limit_bytes=V7X_VMEM_LIMIT),
        name="v_weight_t",
    )(w_in_l)


def _in_proj_kernel(x_ref, mod_ref, g1_ref, w_ref, wvt_ref, gq_ref, gk_ref,
                    cda_ref, sda_a_ref, sda_b_ref, cr_ref, sr_ref,
                    qa_ref, ka_ref, vt_ref, qkr_ref, vr_ref, ga_ref, gbs_ref):
    tm = x_ref.shape[0]
    x = x_ref[...]
    ms = jnp.mean(x * x, axis=-1, keepdims=True)
    h = x * lax.rsqrt(ms + EPS) * g1_ref[...]
    mod = mod_ref[0]
    hm = (h * (1.0 + mod[1:2, :]) + mod[0:1, :]).astype(BF16)

    def proj(col, width):
        return jnp.dot(hm, w_ref[:, col:col + width], preferred_element_type=F32)

    cda, sda_a, sda_b = cda_ref[...], sda_a_ref[...], sda_b_ref[...]
    first_map = lax.broadcasted_iota(jnp.int32, (tm, LANES), 1) < DA_QK_DIM

    def qk_norm_rope(acc, gain, out_ref):
        for j in range(D_MODEL // LANES):
            a = acc[:, j * LANES:(j + 1) * LANES]
            a2 = a * a
            ms1 = jnp.sum(jnp.where(first_map, a2, 0.0), axis=-1, keepdims=True)
            ms2 = jnp.sum(jnp.where(first_map, 0.0, a2), axis=-1, keepdims=True)
            rs = lax.rsqrt(jnp.where(first_map, ms1, ms2) * (1.0 / DA_QK_DIM) + EPS)
            a = a * rs * gain
            rot = (a * cda + pltpu.roll(a, LANES - 32, 1) * sda_a
                   + pltpu.roll(a, 32, 1) * sda_b)
            out_ref[:, j * LANES:(j + 1) * LANES] = rot.astype(BF16)

    qk_norm_rope(proj(C_QA, D_MODEL), gq_ref[...], qa_ref)
    qk_norm_rope(proj(C_KA, D_MODEL), gk_ref[...], ka_ref)

    vt = lax.dot_general(wvt_ref[...], hm, NT_DIMS, preferred_element_type=F32)
    vt_ref[0, 0] = vt.astype(BF16)

    qkr = proj(C_QR, D_MODEL)
    cr, sr = cr_ref[...], sr_ref[...]
    for j in range(D_MODEL // LANES):
        a = qkr[:, j * LANES:(j + 1) * LANES]
        rot = a * cr + pltpu.roll(a, LANES // 2, 1) * sr
        if j >= RET_HEADS:
            rot = rot * (RET_K_DIM ** -0.5)
        qkr_ref[:, j * LANES:(j + 1) * LANES] = rot.astype(BF16)

    vr_ref[...] = proj(C_VR, D_MODEL).astype(BF16)

    gr = proj(C_GR, D_MODEL)
    ga_ref[...] = jax.nn.sigmoid(proj(C_GA, D_MODEL)).astype(BF16)
    gbs_ref[...] = (jax.nn.sigmoid(proj(C_GB, D_MODEL))
                    * (gr * jax.nn.sigmoid(gr))).astype(BF16)


def _in_proj(x2, mod3, g1, w_in, w_vt, gq, gk, tabs, batch, seq):
    t, d = x2.shape
    tm = TM_PROJ
    tpb = seq // tm
    row = pl.BlockSpec((tm, d), lambda i: (i, 0))
    tab = pl.BlockSpec((tm, LANES), lambda i: (i % tpb, 0))
    vec = pl.BlockSpec((1, LANES), lambda i: (0, 0))
    out_row = jax.ShapeDtypeStruct((t, d), BF16)
    return pl.pallas_call(
        _in_proj_kernel,
        grid=(t // tm,),
        in_specs=[row,
                  pl.BlockSpec((1, N_MOD, d), lambda i: (i // tpb, 0, 0)),
                  pl.BlockSpec((1, d), lambda i: (0, 0)),
                  _resident((d, IN_COLS)),
                  _resident((d, d)),
                  vec, vec, tab, tab, tab, tab, tab],
        out_specs=[row, row,
                   pl.BlockSpec((1, 1, d, tm),
                                lambda i: (i // tpb, (i % tpb) // (TK // tm), 0,
                                           (i % tpb) % (TK // tm))),
                   row, row, row, row],
        out_shape=[out_row, out_row,
                   jax.ShapeDtypeStruct((batch, seq // TK, d, TK), BF16),
                   out_row, out_row, out_row, out_row],
        compiler_params=pltpu.CompilerParams(
            dimension_semantics=("arbitrary",),
            vmem_limit_bytes=V7X_VMEM_LIMIT),
        name="in_proj",
    )(x2, mod3, g1, w_in, w_vt, gq, gk, *tabs)


def _diff_attn_kernel(q_ref, k_ref, vt_ref, bias_ref, lam_ref, g_ref, o_ref,
                      sa_ref, sb_ref, acc_ref, l_ref, m_ref, bound_ref, kmax_ref):
    qi = pl.program_id(2)
    heads = range(ATTN_HEADS_PER_STEP)
    lane = lax.broadcasted_iota(jnp.int32, (TQ, LANES), 1)
    qcat = []
    for hd in heads:
        q = q_ref[0, :, hd * LANES:(hd + 1) * LANES]
        zero = jnp.zeros_like(q)
        qcat.append(jnp.concatenate([jnp.where(lane < DA_QK_DIM, q, zero),
                                     jnp.where(lane >= DA_QK_DIM, q, zero)], axis=0))

    def k_block(j, hd):
        koff = pl.multiple_of(j * TK, TK)
        return k_ref[0, pl.ds(koff, TK), hd * LANES:(hd + 1) * LANES]

    def v_block(j, hd):
        return vt_ref[0, j, hd * DA_V_DIM:(hd + 1) * DA_V_DIM, :]

    @pl.when(qi == 0)
    def _():
        gi = lax.broadcasted_iota(jnp.int32, (LANES, LANES), 0) // DA_QK_DIM
        gj = lax.broadcasted_iota(jnp.int32, (LANES, LANES), 1) // DA_QK_DIM
        group_sum = jnp.where(gi == gj, 1.0, 0.0).astype(BF16)
        for hd in heads:
            k = k_ref[0, :, hd * LANES:(hd + 1) * LANES].astype(F32)
            norm2 = jnp.dot((k * k).astype(BF16), group_sum, preferred_element_type=F32)
            kmax_ref[hd] = jnp.max(norm2, axis=0, keepdims=True)

    def set_bounds():
        for hd in heads:
            qsq = jnp.square(qcat[hd].astype(F32)).astype(BF16)
            kmax = jnp.broadcast_to(kmax_ref[hd], (16, LANES)).astype(BF16)
            ub2 = lax.dot_general(kmax, qsq, NT_DIMS, preferred_element_type=F32)
            bound_ref[hd] = jnp.sqrt(ub2[0:1, :] * BOUND_MARGIN)

    def stream(blocks, with_diagonal):
        for hd in heads:
            pv, row_sum = None, None
            for n, j in enumerate(blocks):
                s = lax.dot_general(k_block(j, hd), qcat[hd], NT_DIMS,
                                    preferred_element_type=F32)
                if with_diagonal and n == len(blocks) - 1:
                    s = s + bias_ref[...]
                e = jnp.exp2(s - bound_ref[hd])
                part_sum = jnp.sum(e, axis=0, keepdims=True)
                part = jnp.dot(v_block(j, hd), e.astype(BF16),
                               preferred_element_type=F32)
                pv = part if pv is None else pv + part
                row_sum = part_sum if row_sum is None else row_sum + part_sum
            acc_ref[hd] = pv if with_diagonal else acc_ref[hd] + pv
            l_ref[hd] = row_sum if with_diagonal else l_ref[hd] + row_sum

    @pl.when(qi % 2 == 0)
    def _():
        set_bounds()
        stream([qi], True)

    @pl.when(qi % 2 == 1)
    def _():
        set_bounds()
        stream([qi - 1, qi], True)

    def stream_body(t, carry):
        stream([2 * t, 2 * t + 1], False)
        return carry

    lax.fori_loop(0, qi // 2, stream_body, 0)

    def finalize():
        for hd in heads:
            num = acc_ref[hd] * (1.0 / l_ref[hd])
            o = num[:, :TQ] - lam_ref[...] * num[:, TQ:]
            ms = jnp.mean(o * o, axis=0, keepdims=True)
            on = o * lax.rsqrt(ms + EPS) * g_ref[...]
            o_ref[0, :, hd * LANES:(hd + 1) * LANES] = on.T.astype(BF16)

    finalize()
    l_min = jnp.min(l_ref[...])

    def scores(j, s_ref):
        for hd in heads:
            s_ref[hd] = lax.dot_general(k_block(j, hd), qcat[hd], NT_DIMS,
                                        preferred_element_type=F32)

    def softmax_pv(j, s_ref, masked):
        for hd in heads:
            s = s_ref[hd]
            if masked:
                s = s + bias_ref[...]
            m_old = m_ref[hd]
            m_new = jnp.maximum(m_old, jnp.max(s, axis=0, keepdims=True))
            alpha = jnp.exp2(m_old - m_new)
            e = jnp.exp2(s - m_new)
            pv = jnp.dot(v_block(j, hd), e.astype(BF16), preferred_element_type=F32)
            acc_ref[hd] = alpha * acc_ref[hd] + pv
            l_ref[hd] = alpha * l_ref[hd] + jnp.sum(e, axis=0, keepdims=True)
            m_ref[hd] = m_new

    @pl.when(jnp.logical_not(l_min >= ROW_SUM_FLOOR))
    def _():
        m_ref[...] = jnp.full(m_ref.shape, -jnp.inf, F32)
        acc_ref[...] = jnp.zeros(acc_ref.shape, F32)
        l_ref[...] = jnp.zeros(l_ref.shape, F32)
        scores(0, sa_ref)

        def pair(t, carry):
            scores(2 * t + 1, sb_ref)
            softmax_pv(2 * t, sa_ref, False)
            scores(2 * t + 2, sa_ref)
            softmax_pv(2 * t + 1, sb_ref, False)
            return carry

        lax.fori_loop(0, qi // 2, pair, 0)

        @pl.when(qi % 2 == 0)
        def _():
            softmax_pv(qi, sa_ref, True)

        @pl.when(qi % 2 == 1)
        def _():
            scores(qi, sb_ref)
            softmax_pv(qi - 1, sa_ref, False)
            softmax_pv(qi, sb_ref, True)

        finalize()


def _diag_bias():
    kc = np.arange(TK)[:, None] // CHUNK
    qc = (np.arange(2 * TQ)[None, :] % TQ) // CHUNK
    return jnp.asarray(np.where(kc <= qc, 0.0, -np.inf).astype(np.float32))


def _diff_attn(qa, ka, vt, lam, g_col, batch, seq):
    hps = ATTN_HEADS_PER_STEP
    nq = seq // TQ
    return pl.pallas_call(
        _diff_attn_kernel,
        grid=(batch, DA_HEADS // hps, nq),
        in_specs=[pl.BlockSpec((1, TQ, hps * LANES), lambda b, h, i: (b, i, h)),
                  pl.BlockSpec((1, seq, hps * LANES), lambda b, h, i: (b, 0, h)),
                  pl.BlockSpec((1, seq // TK, hps * DA_V_DIM, TK), lambda b, h, i: (b, 0, h, 0)),
                  _resident((TK, 2 * TQ)),
                  pl.BlockSpec((1, 1), lambda b, h, i: (0, 0)),
                  pl.BlockSpec((DA_V_DIM, 1), lambda b, h, i: (0, 0))],
        out_specs=pl.BlockSpec((1, TQ, hps * LANES), lambda b, h, i: (b, i, h)),
        out_shape=jax.ShapeDtypeStruct((batch, seq, D_MODEL), BF16),
        scratch_shapes=[pltpu.VMEM((hps, TK, 2 * TQ), F32),
                        pltpu.VMEM((hps, TK, 2 * TQ), F32),
                        pltpu.VMEM((hps, DA_V_DIM, 2 * TQ), F32),
                        pltpu.VMEM((hps, 1, 2 * TQ), F32),
                        pltpu.VMEM((hps, 1, 2 * TQ), F32),
                        pltpu.VMEM((hps, 1, 2 * TQ), F32),
                        pltpu.VMEM((hps, 1, LANES), F32)],
        compiler_params=pltpu.CompilerParams(
            dimension_semantics=("arbitrary", "arbitrary", "arbitrary"),
            vmem_limit_bytes=V7X_VMEM_LIMIT),
        name="diff_attn",
    )(qa, ka, vt, _diag_bias(), lam, g_col)


def _retention_kernel(qk_ref, v_ref, g_ref, intra_ref, inner_ref, kvd_ref, o_ref,
                      state_ref, *, chunk_decay):
    @pl.when(pl.program_id(1) == 0)
    def _():
        state_ref[...] = jnp.zeros(state_ref.shape, F32)

    cr = intra_ref.shape[1]
    for ci in range(RET_CHUNKS_PER_STEP):
        rows = slice(ci * cr, (ci + 1) * cr)
        for hd in range(RET_HEADS):
            q = qk_ref[0, rows, hd * RET_K_DIM:(hd + 1) * RET_K_DIM]
            k = qk_ref[0, rows, (RET_HEADS + hd) * RET_K_DIM:(RET_HEADS + hd + 1) * RET_K_DIM]
            v = v_ref[0, rows, hd * RET_V_DIM:(hd + 1) * RET_V_DIM]
            sc = lax.dot_general(q, k, NT_DIMS, preferred_element_type=F32) * intra_ref[hd]
            y = jnp.dot(sc.astype(BF16), v, preferred_element_type=F32)
            st = state_ref[hd]
            cross = jnp.dot(q, st.astype(BF16), preferred_element_type=F32)
            y = y + cross * inner_ref[hd]
            kd = (k.astype(F32) * kvd_ref[hd]).astype(BF16)
            upd = lax.dot_general(kd, v, TN_DIMS, preferred_element_type=F32)
            state_ref[hd] = st * chunk_decay[hd] + upd
            ms = jnp.mean(y * y, axis=-1, keepdims=True)
            o_ref[0, rows, hd * RET_V_DIM:(hd + 1) * RET_V_DIM] = (
                y * lax.rsqrt(ms + EPS) * g_ref[...]).astype(BF16)


def _retention_tables(log_gamma, cr):
    n = np.arange(cr, dtype=np.float32)
    lg = log_gamma.astype(np.float32)[:, None, None]
    diff = n[:, None] - n[None, :]
    intra = np.where(diff >= 0, np.exp(np.maximum(diff, 0.0) * lg), 0.0).astype(np.float32)
    inner = np.exp((n + 1.0)[None, :, None] * lg).astype(np.float32)
    kvd = np.exp((cr - 1.0 - n)[None, :, None] * lg).astype(np.float32)
    inner = np.broadcast_to(inner, (RET_HEADS, cr, RET_V_DIM))
    kvd = np.broadcast_to(kvd, (RET_HEADS, cr, RET_K_DIM))
    chunk = tuple(float(np.exp(np.float32(cr) * g)) for g in log_gamma.astype(np.float32))
    return jnp.asarray(intra), jnp.asarray(inner), jnp.asarray(kvd), chunk


def _retention(qkr, vr, g_ret, log_gamma, batch, seq):
    cr = RET_CHUNK
    intra, inner, kvd, chunk_decay = _retention_tables(log_gamma, cr)
    step_rows = cr * RET_CHUNKS_PER_STEP
    rows = pl.BlockSpec((1, step_rows, D_MODEL), lambda b, c: (b, c, 0))
    return pl.pallas_call(
        functools.partial(_retention_kernel, chunk_decay=chunk_decay),
        grid=(batch, seq // step_rows),
        in_specs=[rows, rows,
                  pl.BlockSpec((1, RET_V_DIM), lambda b, c: (0, 0)),
                  _resident((RET_HEADS, cr, cr)),
                  _resident((RET_HEADS, cr, RET_V_DIM)),
                  _resident((RET_HEADS, cr, RET_K_DIM))],
        out_specs=rows,
        out_shape=jax.ShapeDtypeStruct((batch, seq, D_MODEL), BF16),
        scratch_shapes=[pltpu.VMEM((RET_HEADS, RET_K_DIM, RET_V_DIM), F32)],
        compiler_params=pltpu.CompilerParams(
            dimension_semantics=("arbitrary", "arbitrary"),
            vmem_limit_bytes=V7X_VMEM_LIMIT),
        name="retention",
    )(qkr, vr, g_ret, intra, inner, kvd)


def _out_mlp_kernel(x_ref, oa_ref, or_ref, ga_ref, gbs_ref, mod_ref, g2_ref,
                    wo_ref, wu_ref, wd_ref, out_ref):
    mod = mod_ref[0]
    u = (ga_ref[...].astype(F32) * oa_ref[...].astype(F32)
         + gbs_ref[...].astype(F32) * or_ref[...].astype(F32)).astype(BF16)
    y = jnp.dot(u, wo_ref[...], preferred_element_type=F32)
    x1 = x_ref[...] + mod[2:3, :] * y
    ms = jnp.mean(x1 * x1, axis=-1, keepdims=True)
    hn = x1 * lax.rsqrt(ms + EPS) * g2_ref[...]
    hff = (hn * (1.0 + mod[4:5, :]) + mod[3:4, :]).astype(BF16)
    up = jnp.dot(hff, wu_ref[...], preferred_element_type=F32)
    act = jnp.square(jnp.maximum(up, 0.0)).astype(BF16)
    ff = jnp.dot(act, wd_ref[...], preferred_element_type=F32)
    out_ref[...] = x1 + mod[5:6, :] * ff


def _out_mlp(x2, oa, orr, ga, gbs, mod3, g2, w_out, w_up, w_down, seq):
    t, d = x2.shape
    tm = TM_MLP
    tpb = seq // tm
    row = pl.BlockSpec((tm, d), lambda i: (i, 0))
    return pl.pallas_call(
        _out_mlp_kernel,
        grid=(t // tm,),
        in_specs=[row, row, row, row, row,
                  pl.BlockSpec((1, N_MOD, d), lambda i: (i // tpb, 0, 0)),
                  pl.BlockSpec((1, d), lambda i: (0, 0)),
                  _resident((d, d)), _resident((d, D_FF)), _resident((D_FF, d))],
        out_specs=row,
        out_shape=jax.ShapeDtypeStruct((t, d), F32),
        compiler_params=pltpu.CompilerParams(
            dimension_semantics=("arbitrary",),
            vmem_limit_bytes=V7X_VMEM_LIMIT),
        name="out_mlp",
    )(x2, oa, orr, ga, gbs, mod3, g2, w_out, w_up, w_down)


def _rope_tables(seq):
    pos = jnp.arange(seq, dtype=F32)[:, None]
    da_inv = ROPE_THETA ** (-jnp.arange(0, DA_QK_DIM, 2, dtype=F32) / DA_QK_DIM)
    ang = pos * da_inv[None, :]
    cos, sin = jnp.cos(ang), jnp.sin(ang)
    zero = jnp.zeros_like(sin)
    cda = jnp.tile(cos, (1, 4))
    sda_a = jnp.tile(jnp.concatenate([-sin, zero], axis=1), (1, 2))
    sda_b = jnp.tile(jnp.concatenate([zero, sin], axis=1), (1, 2))
    ret_inv = 1.0 / (ROPE_THETA ** jnp.linspace(0.0, 1.0, RET_K_DIM // 2, dtype=F32))
    ang_r = pos * ret_inv[None, :]
    cos_r, sin_r = jnp.cos(ang_r), jnp.sin(ang_r)
    cr = jnp.concatenate([cos_r, cos_r], axis=1)
    sr = jnp.concatenate([-sin_r, sin_r], axis=1)
    return cda, sda_a, sda_b, cr, sr


def kernel(x, c, w_ada, b_ada, g_norm1, w_in, g_q, g_k, lambda_q1, lambda_k1,
           lambda_q2, lambda_k2, g_da_out, g_ret_out, w_out, g_norm2, w_up, w_down):
    batch, seq, d = x.shape
    depth = w_ada.shape[0]
    log_gamma = np.log(1.0 - 2.0 ** (-5.0 - np.arange(RET_HEADS))).astype(np.float32)
    tabs = _rope_tables(seq)

    for l in range(depth):
        lambda_init = 0.8 - 0.6 * math.exp(-0.3 * l)
        mod, lam = _modulation(c, w_ada[l], b_ada[l][None, :],
                               lambda_q1[l][None, :], lambda_k1[l][None, :],
                               lambda_q2[l][None, :], lambda_k2[l][None, :],
                               lambda_init)
        mod3 = mod.reshape(batch, N_MOD, d)
        x2 = x.reshape(batch * seq, d)

        w_in_b = w_in[l].astype(BF16)
        w_vt = _v_weight_transposed(w_in[l])
        gq = (jnp.tile(g_q[l], 2) * (DA_QK_DIM ** -0.5 * LOG2E))[None, :]
        gk = jnp.tile(g_k[l], 2)[None, :]
        qa, ka, vt, qkr, vr, ga, gbs = _in_proj(
            x2, mod3, g_norm1[l][None, :], w_in_b, w_vt, gq, gk, tabs, batch, seq)

        g_col = (g_da_out[l] * (1.0 - lambda_init))[:, None]
        oa = _diff_attn(qa.reshape(batch, seq, d), ka.reshape(batch, seq, d), vt,
                        lam, g_col, batch, seq)
        orr = _retention(qkr.reshape(batch, seq, d), vr.reshape(batch, seq, d),
                         g_ret_out[l][None, :], log_gamma, batch, seq)

        out = _out_mlp(x2, oa.reshape(batch * seq, d), orr.reshape(batch * seq, d),
                       ga, gbs, mod3, g_norm2[l][None, :],
                       w_out[l].astype(BF16), w_up[l].astype(BF16),
                       w_down[l].astype(BF16), seq)
        x = out.reshape(batch, seq, d)
    return x
```

```python
import functools
import math

import numpy as np
import jax
import jax.numpy as jnp
from jax import lax
from jax.experimental import pallas as pl
from jax.experimental.pallas import tpu as pltpu

D_MODEL = 1024
CHUNK = 64
ROPE_THETA = 10000.0
EPS = 1e-6
DA_HEADS = 8
DA_V_DIM = D_MODEL // DA_HEADS
DA_QK_DIM = DA_V_DIM // 2
RET_HEADS = 4
RET_V_DIM = D_MODEL // RET_HEADS
RET_K_DIM = RET_V_DIM // 2
D_FF = 4 * D_MODEL
N_MOD = 6
IN_COLS = 8 * D_MODEL

C_QA, C_KA, C_VA, C_QR, C_KR, C_VR, C_GR, C_GA, C_GB = (
    0, 1024, 2048, 3072, 3584, 4096, 5120, 6144, 7168)

LANES = 128
V7X_VMEM_LIMIT = 56 * 1024 * 1024

TM_PROJ = 512
TQ = 512
TK = 512
ATTN_HEADS_PER_STEP = 4
BOUND_MARGIN = 1.03
ROW_SUM_FLOOR = 2.0 ** -60
LOG2E = math.log2(math.e)
RET_CHUNK = 256
RET_CHUNKS_PER_STEP = 4
TM_MLP = 512

F32 = jnp.float32
BF16 = jnp.bfloat16
NT_DIMS = (((1,), (1,)), ((), ()))
TN_DIMS = (((0,), (0,)), ((), ()))


def _resident(shape):
    return pl.BlockSpec(shape, lambda *_: (0,) * len(shape),
                        pipeline_mode=pl.Buffered(1))


def _mod_kernel(c_ref, w_ref, b_ref, lq1_ref, lk1_ref, lq2_ref, lk2_ref,
                mod_ref, lam_ref, *, lambda_init):
    c = c_ref[...]
    sc = (c * jax.nn.sigmoid(c)).astype(BF16)
    mod_ref[...] = jnp.dot(sc, w_ref[...].astype(BF16),
                           preferred_element_type=F32) + b_ref[...]
    s1 = jnp.sum(lq1_ref[...] * lk1_ref[...], axis=-1, keepdims=True)
    s2 = jnp.sum(lq2_ref[...] * lk2_ref[...], axis=-1, keepdims=True)
    lam_ref[...] = jnp.exp(s1) - jnp.exp(s2) + lambda_init


def _modulation(c, w_ada, b_ada, lq1, lk1, lq2, lk2, lambda_init):
    b, d = c.shape
    n = w_ada.shape[1]
    bn = 1536
    vec = pl.BlockSpec((1, DA_QK_DIM), lambda j: (0, 0))
    return pl.pallas_call(
        functools.partial(_mod_kernel, lambda_init=lambda_init),
        grid=(n // bn,),
        in_specs=[pl.BlockSpec((b, d), lambda j: (0, 0)),
                  pl.BlockSpec((d, bn), lambda j: (0, j)),
                  pl.BlockSpec((1, bn), lambda j: (0, j)),
                  vec, vec, vec, vec],
        out_specs=[pl.BlockSpec((b, bn), lambda j: (0, j)),
                   pl.BlockSpec((1, 1), lambda j: (0, 0))],
        out_shape=[jax.ShapeDtypeStruct((b, n), F32),
                   jax.ShapeDtypeStruct((1, 1), F32)],
        compiler_params=pltpu.CompilerParams(
            dimension_semantics=("arbitrary",),
            vmem_limit_bytes=V7X_VMEM_LIMIT),
        name="modulation",
    )(c, w_ada, b_ada, lq1, lk1, lq2, lk2)


def _transpose_cast_kernel(w_ref, o_ref):
    o_ref[...] = w_ref[...].T.astype(BF16)


def _v_weight_transposed(w_in_l):
    d = w_in_l.shape[0]
    return pl.pallas_call(
        _transpose_cast_kernel,
        grid=(1,),
        in_specs=[pl.BlockSpec((d, d), lambda i: (0, C_VA // d))],
        out_specs=pl.BlockSpec((d, d), lambda i: (0, 0)),
        out_shape=jax.ShapeDtypeStruct((d, d), BF16),
        compiler_params=pltpu.CompilerParams(
            dimension_semantics=("arbitrary",),
            vmem_limit_bytes=V7X_VMEM_LIMIT),
        name="v_weight_t",
    )(w_in_l)


def _in_proj_kernel(x_ref, mod_ref, g1_ref, w_ref, wvt_ref, gq_ref, gk_ref,
                    cda_ref, sda_a_ref, sda_b_ref, cr_ref, sr_ref,
                    qa_ref, ka_ref, vt_ref, qkr_ref, vr_ref, ga_ref, gbs_ref):
    tm = x_ref.shape[0]
    x = x_ref[...]
    ms = jnp.mean(x * x, axis=-1, keepdims=True)
    h = x * lax.rsqrt(ms + EPS) * g1_ref[...]
    mod = mod_ref[0]
    hm = (h * (1.0 + mod[1:2, :]) + mod[0:1, :]).astype(BF16)

    def proj(col, width):
        return jnp.dot(hm, w_ref[:, col:col + width], preferred_element_type=F32)

    cda, sda_a, sda_b = cda_ref[...], sda_a_ref[...], sda_b_ref[...]
    first_map = lax.broadcasted_iota(jnp.int32, (tm, LANES), 1) < DA_QK_DIM

    def qk_norm_rope(acc, gain, out_ref):
        for j in range(D_MODEL // LANES):
            a = acc[:, j * LANES:(j + 1) * LANES]
            a2 = a * a
            ms1 = jnp.sum(jnp.where(first_map, a2, 0.0), axis=-1, keepdims=True)
            ms2 = jnp.sum(jnp.where(first_map, 0.0, a2), axis=-1, keepdims=True)
            rs = lax.rsqrt(jnp.where(first_map, ms1, ms2) * (1.0 / DA_QK_DIM) + EPS)
            a = a * rs * gain
            rot = (a * cda + pltpu.roll(a, LANES - 32, 1) * sda_a
                   + pltpu.roll(a, 32, 1) * sda_b)
            out_ref[:, j * LANES:(j + 1) * LANES] = rot.astype(BF16)

    qk_norm_rope(proj(C_QA, D_MODEL), gq_ref[...], qa_ref)
    qk_norm_rope(proj(C_KA, D_MODEL), gk_ref[...], ka_ref)

    vt = lax.dot_general(wvt_ref[...], hm, NT_DIMS, preferred_element_type=F32)
    vt_ref[0, 0] = vt.astype(BF16)

    qkr = proj(C_QR, D_MODEL)
    cr, sr = cr_ref[...], sr_ref[...]
    for j in range(D_MODEL // LANES):
        a = qkr[:, j * LANES:(j + 1) * LANES]
        rot = a * cr + pltpu.roll(a, LANES // 2, 1) * sr
        if j >= RET_HEADS:
            rot = rot * (RET_K_DIM ** -0.5)
        qkr_ref[:, j * LANES:(j + 1) * LANES] = rot.astype(BF16)

    vr_ref[...] = proj(C_VR, D_MODEL).astype(BF16)

    gr = proj(C_GR, D_MODEL)
    ga_ref[...] = jax.nn.sigmoid(proj(C_GA, D_MODEL)).astype(BF16)
    gbs_ref[...] = (jax.nn.sigmoid(proj(C_GB, D_MODEL))
                    * (gr * jax.nn.sigmoid(gr))).astype(BF16)


def _in_proj(x2, mod3, g1, w_in, w_vt, gq, gk, tabs, batch, seq):
    t, d = x2.shape
    tm = TM_PROJ
    tpb = seq // tm
    row = pl.BlockSpec((tm, d), lambda i: (i, 0))
    tab = pl.BlockSpec((tm, LANES), lambda i: (i % tpb, 0))
    vec = pl.BlockSpec((1, LANES), lambda i: (0, 0))
    out_row = jax.ShapeDtypeStruct((t, d), BF16)
    return pl.pallas_call(
        _in_proj_kernel,
        grid=(t // tm,),
        in_specs=[row,
                  pl.BlockSpec((1, N_MOD, d), lambda i: (i // tpb, 0, 0)),
                  pl.BlockSpec((1, d), lambda i: (0, 0)),
                  _resident((d, IN_COLS)),
                  _resident((d, d)),
                  vec, vec, tab, tab, tab, tab, tab],
        out_specs=[row, row,
                   pl.BlockSpec((1, 1, d, tm),
                                lambda i: (i // tpb, (i % tpb) // (TK // tm), 0,
                                           (i % tpb) % (TK // tm))),
                   row, row, row, row],
        out_shape=[out_row, out_row,
                   jax.ShapeDtypeStruct((batch, seq // TK, d, TK), BF16),
                   out_row, out_row, out_row, out_row],
        compiler_params=pltpu.CompilerParams(
            dimension_semantics=("arbitrary",),
            vmem_limit_bytes=V7X_VMEM_LIMIT),
        name="in_proj",
    )(x2, mod3, g1, w_in, w_vt, gq, gk, *tabs)


def _diff_attn_kernel(q_ref, k_ref, vt_ref, bias_ref, lam_ref, g_ref, o_ref,
                      acc_ref, l_ref, m_ref, bound_ref, kmax_ref):
    qi = pl.program_id(2)
    heads = range(ATTN_HEADS_PER_STEP)
    lane = lax.broadcasted_iota(jnp.int32, (TQ, LANES), 1)
    qcat = []
    for hd in heads:
        q = q_ref[0, :, hd * LANES:(hd + 1) * LANES]
        zero = jnp.zeros_like(q)
        qcat.append(jnp.concatenate([jnp.where(lane < DA_QK_DIM, q, zero),
                                     jnp.where(lane >= DA_QK_DIM, q, zero)], axis=0))

    def k_block(j, hd):
        koff = pl.multiple_of(j * TK, TK)
        return k_ref[0, pl.ds(koff, TK), hd * LANES:(hd + 1) * LANES]

    def v_block(j, hd):
        return vt_ref[0, j, hd * DA_V_DIM:(hd + 1) * DA_V_DIM, :]

    @pl.when(qi == 0)
    def _():
        gi = lax.broadcasted_iota(jnp.int32, (LANES, LANES), 0) // DA_QK_DIM
        gj = lax.broadcasted_iota(jnp.int32, (LANES, LANES), 1) // DA_QK_DIM
        group_sum = jnp.where(gi == gj, 1.0, 0.0).astype(BF16)
        for hd in heads:
            k = k_ref[0, :, hd * LANES:(hd + 1) * LANES].astype(F32)
            norm2 = jnp.dot((k * k).astype(BF16), group_sum, preferred_element_type=F32)
            kmax_ref[hd] = jnp.max(norm2, axis=0, keepdims=True)

    def set_bounds():
        for hd in heads:
            qsq = jnp.square(qcat[hd].astype(F32)).astype(BF16)
            kmax = jnp.broadcast_to(kmax_ref[hd], (16, LANES)).astype(BF16)
            ub2 = lax.dot_general(kmax, qsq, NT_DIMS, preferred_element_type=F32)
            bound_ref[hd] = jnp.sqrt(ub2[0:1, :] * BOUND_MARGIN)

    def stream(blocks, with_diagonal):
        for hd in heads:
            pv, row_sum = None, None
            for n, j in enumerate(blocks):
                s = lax.dot_general(k_block(j, hd), qcat[hd], NT_DIMS,
                                    preferred_element_type=F32)
                if with_diagonal and n == len(blocks) - 1:
                    s = s + bias_ref[...]
                e = jnp.exp2(s - bound_ref[hd])
                part_sum = jnp.sum(e, axis=0, keepdims=True)
                part = jnp.dot(v_block(j, hd), e.astype(BF16),
                               preferred_element_type=F32)
                pv = part if pv is None else pv + part
                row_sum = part_sum if row_sum is None else row_sum + part_sum
            acc_ref[hd] = pv if with_diagonal else acc_ref[hd] + pv
            l_ref[hd] = row_sum if with_diagonal else l_ref[hd] + row_sum

    @pl.when(qi % 2 == 0)
    def _():
        set_bounds()
        stream([qi], True)

    @pl.when(qi % 2 == 1)
    def _():
        set_bounds()
        stream([qi - 1, qi], True)

    def stream_body(t, carry):
        stream([2 * t, 2 * t + 1], False)
        return carry

    lax.fori_loop(0, qi // 2, stream_body, 0)

    def finalize():
        for hd in heads:
            num = acc_ref[hd] * (1.0 / l_ref[hd])
            o = num[:, :TQ] - lam_ref[...] * num[:, TQ:]
            ms = jnp.mean(o * o, axis=0, keepdims=True)
            on = o * lax.rsqrt(ms + EPS) * g_ref[...]
            o_ref[0, :, hd * LANES:(hd + 1) * LANES] = on.T.astype(BF16)

    finalize()
    l_min = jnp.min(l_ref[...])

    def online_step(j, carry):
        for hd in heads:
            s = lax.dot_general(k_block(j, hd), qcat[hd], NT_DIMS,
                                preferred_element_type=F32)
            s = s + jnp.where(j == qi, bias_ref[...], 0.0)
            m_old = m_ref[hd]
            m_new = jnp.maximum(m_old, jnp.max(s, axis=0, keepdims=True))
            alpha = jnp.exp2(m_old - m_new)
            e = jnp.exp2(s - m_new)
            pv = jnp.dot(v_block(j, hd), e.astype(BF16), preferred_element_type=F32)
            acc_ref[hd] = alpha * acc_ref[hd] + pv
            l_ref[hd] = alpha * l_ref[hd] + jnp.sum(e, axis=0, keepdims=True)
            m_ref[hd] = m_new
        return carry

    @pl.when(jnp.logical_not(l_min >= ROW_SUM_FLOOR))
    def _():
        m_ref[...] = jnp.full(m_ref.shape, -jnp.inf, F32)
        acc_ref[...] = jnp.zeros(acc_ref.shape, F32)
        l_ref[...] = jnp.zeros(l_ref.shape, F32)
        lax.fori_loop(0, qi + 1, online_step, 0)
        finalize()


def _diag_bias():
    kc = np.arange(TK)[:, None] // CHUNK
    qc = (np.arange(2 * TQ)[None, :] % TQ) // CHUNK
    return jnp.asarray(np.where(kc <= qc, 0.0, -np.inf).astype(np.float32))


def _diff_attn(qa, ka, vt, lam, g_col, batch, seq):
    hps = ATTN_HEADS_PER_STEP
    nq = seq // TQ
    return pl.pallas_call(
        _diff_attn_kernel,
        grid=(batch, DA_HEADS // hps, nq),
        in_specs=[pl.BlockSpec((1, TQ, hps * LANES), lambda b, h, i: (b, i, h)),
                  pl.BlockSpec((1, seq, hps * LANES), lambda b, h, i: (b, 0, h)),
                  pl.BlockSpec((1, seq // TK, hps * DA_V_DIM, TK), lambda b, h, i: (b, 0, h, 0)),
                  _resident((TK, 2 * TQ)),
                  pl.BlockSpec((1, 1), lambda b, h, i: (0, 0)),
                  pl.BlockSpec((DA_V_DIM, 1), lambda b, h, i: (0, 0))],
        out_specs=pl.BlockSpec((1, TQ, hps * LANES), lambda b, h, i: (b, i, h)),
        out_shape=jax.ShapeDtypeStruct((batch, seq, D_MODEL), BF16),
        scratch_shapes=[pltpu.VMEM((hps, DA_V_DIM, 2 * TQ), F32),
                        pltpu.VMEM((hps, 1, 2 * TQ), F32),
                        pltpu.VMEM((hps, 1, 2 * TQ), F32),
                        pltpu.VMEM((hps, 1, 2 * TQ), F32),
                        pltpu.VMEM((hps, 1, LANES), F32)],
        compiler_params=pltpu.CompilerParams(
            dimension_semantics=("arbitrary", "arbitrary", "arbitrary"),
            vmem_limit_bytes=V7X_VMEM_LIMIT),
        name="diff_attn",
    )(qa, ka, vt, _diag_bias(), lam, g_col)


def _retention_kernel(qk_ref, v_ref, g_ref, intra_ref, inner_ref, kvd_ref, o_ref,
                      state_ref, *, chunk_decay):
    @pl.when(pl.program_id(1) == 0)
    def _():
        state_ref[...] = jnp.zeros(state_ref.shape, F32)

    cr = intra_ref.shape[1]
    for ci in range(RET_CHUNKS_PER_STEP):
        rows = slice(ci * cr, (ci + 1) * cr)
        for hd in range(RET_HEADS):
            q = qk_ref[0, rows, hd * RET_K_DIM:(hd + 1) * RET_K_DIM]
            k = qk_ref[0, rows, (RET_HEADS + hd) * RET_K_DIM:(RET_HEADS + hd + 1) * RET_K_DIM]
            v = v_ref[0, rows, hd * RET_V_DIM:(hd + 1) * RET_V_DIM]
            sc = lax.dot_general(q, k, NT_DIMS, preferred_element_type=F32) * intra_ref[hd]
            y = jnp.dot(sc.astype(BF16), v, preferred_element_type=F32)
            st = state_ref[hd]
            cross = jnp.dot(q, st.astype(BF16), preferred_element_type=F32)
            y = y + cross * inner_ref[hd]
            kd = (k.astype(F32) * kvd_ref[hd]).astype(BF16)
            upd = lax.dot_general(kd, v, TN_DIMS, preferred_element_type=F32)
            state_ref[hd] = st * chunk_decay[hd] + upd
            ms = jnp.mean(y * y, axis=-1, keepdims=True)
            o_ref[0, rows, hd * RET_V_DIM:(hd + 1) * RET_V_DIM] = (
                y * lax.rsqrt(ms + EPS) * g_ref[...]).astype(BF16)


def _retention_tables(log_gamma, cr):
    n = np.arange(cr, dtype=np.float32)
    lg = log_gamma.astype(np.float32)[:, None, None]
    diff = n[:, None] - n[None, :]
    intra = np.where(diff >= 0, np.exp(np.maximum(diff, 0.0) * lg), 0.0).astype(np.float32)
    inner = np.exp((n + 1.0)[None, :, None] * lg).astype(np.float32)
    kvd = np.exp((cr - 1.0 - n)[None, :, None] * lg).astype(np.float32)
    inner = np.broadcast_to(inner, (RET_HEADS, cr, RET_V_DIM))
    kvd = np.broadcast_to(kvd, (RET_HEADS, cr, RET_K_DIM))
    chunk = tuple(float(np.exp(np.float32(cr) * g)) for g in log_gamma.astype(np.float32))
    return jnp.asarray(intra), jnp.asarray(inner), jnp.asarray(kvd), chunk


def _retention(qkr, vr, g_ret, log_gamma, batch, seq):
    cr = RET_CHUNK
    intra, inner, kvd, chunk_decay = _retention_tables(log_gamma, cr)
    step_rows = cr * RET_CHUNKS_PER_STEP
    rows = pl.BlockSpec((1, step_rows, D_MODEL), lambda b, c: (b, c, 0))
    return pl.pallas_call(
        functools.partial(_retention_kernel, chunk_decay=chunk_decay),
        grid=(batch, seq // step_rows),
        in_specs=[rows, rows,
                  pl.BlockSpec((1, RET_V_DIM), lambda b, c: (0, 0)),
                  _resident((RET_HEADS, cr, cr)),
                  _resident((RET_HEADS, cr, RET_V_DIM)),
                  _resident((RET_HEADS, cr, RET_K_DIM))],
        out_specs=rows,
        out_shape=jax.ShapeDtypeStruct((batch, seq, D_MODEL), BF16),
        scratch_shapes=[pltpu.VMEM((RET_HEADS, RET_K_DIM, RET_V_DIM), F32)],
        compiler_params=pltpu.CompilerParams(
            dimension_semantics=("arbitrary", "arbitrary"),
            vmem_limit_bytes=V7X_VMEM_LIMIT),
        name="retention",
    )(qkr, vr, g_ret, intra, inner, kvd)


def _out_mlp_kernel(x_ref, oa_ref, or_ref, ga_ref, gbs_ref, mod_ref, g2_ref,
                    wo_ref, wu_ref, wd_ref, out_ref):
    mod = mod_ref[0]
    u = (ga_ref[...].astype(F32) * oa_ref[...].astype(F32)
         + gbs_ref[...].astype(F32) * or_ref[...].astype(F32)).astype(BF16)
    y = jnp.dot(u, wo_ref[...], preferred_element_type=F32)
    x1 = x_ref[...] + mod[2:3, :] * y
    ms = jnp.mean(x1 * x1, axis=-1, keepdims=True)
    hn = x1 * lax.rsqrt(ms + EPS) * g2_ref[...]
    hff = (hn * (1.0 + mod[4:5, :]) + mod[3:4, :]).astype(BF16)
    up = jnp.dot(hff, wu_ref[...], preferred_element_type=F32)
    act = jnp.square(jnp.maximum(up, 0.0)).astype(BF16)
    ff = jnp.dot(act, wd_ref[...], preferred_element_type=F32)
    out_ref[...] = x1 + mod[5:6, :] * ff


def _out_mlp(x2, oa, orr, ga, gbs, mod3, g2, w_out, w_up, w_down, seq):
    t, d = x2.shape
    tm = TM_MLP
    tpb = seq // tm
    row = pl.BlockSpec((tm, d), lambda i: (i, 0))
    return pl.pallas_call(
        _out_mlp_kernel,
        grid=(t // tm,),
        in_specs=[row, row, row, row, row,
                  pl.BlockSpec((1, N_MOD, d), lambda i: (i // tpb, 0, 0)),
                  pl.BlockSpec((1, d), lambda i: (0, 0)),
                  _resident((d, d)), _resident((d, D_FF)), _resident((D_FF, d))],
        out_specs=row,
        out_shape=jax.ShapeDtypeStruct((t, d), F32),
        compiler_params=pltpu.CompilerParams(
            dimension_semantics=("arbitrary",),
            vmem_limit_bytes=V7X_VMEM_LIMIT),
        name="out_mlp",
    )(x2, oa, orr, ga, gbs, mod3, g2, w_out, w_up, w_down)


def _rope_tables(seq):
    pos = jnp.arange(seq, dtype=F32)[:, None]
    da_inv = ROPE_THETA ** (-jnp.arange(0, DA_QK_DIM, 2, dtype=F32) / DA_QK_DIM)
    ang = pos * da_inv[None, :]
    cos, sin = jnp.cos(ang), jnp.sin(ang)
    zero = jnp.zeros_like(sin)
    cda = jnp.tile(cos, (1, 4))
    sda_a = jnp.tile(jnp.concatenate([-sin, zero], axis=1), (1, 2))
    sda_b = jnp.tile(jnp.concatenate([zero, sin], axis=1), (1, 2))
    ret_inv = 1.0 / (ROPE_THETA ** jnp.linspace(0.0, 1.0, RET_K_DIM // 2, dtype=F32))
    ang_r = pos * ret_inv[None, :]
    cos_r, sin_r = jnp.cos(ang_r), jnp.sin(ang_r)
    cr = jnp.concatenate([cos_r, cos_r], axis=1)
    sr = jnp.concatenate([-sin_r, sin_r], axis=1)
    return cda, sda_a, sda_b, cr, sr


def kernel(x, c, w_ada, b_ada, g_norm1, w_in, g_q, g_k, lambda_q1, lambda_k1,
           lambda_q2, lambda_k2, g_da_out, g_ret_out, w_out, g_norm2, w_up, w_down):
    batch, seq, d = x.shape
    depth = w_ada.shape[0]
    log_gamma = np.log(1.0 - 2.0 ** (-5.0 - np.arange(RET_HEADS))).astype(np.float32)
    tabs = _rope_tables(seq)

    for l in range(depth):
        lambda_init = 0.8 - 0.6 * math.exp(-0.3 * l)
        mod, lam = _modulation(c, w_ada[l], b_ada[l][None, :],
                               lambda_q1[l][None, :], lambda_k1[l][None, :],
                               lambda_q2[l][None, :], lambda_k2[l][None, :],
                               lambda_init)
        mod3 = mod.reshape(batch, N_MOD, d)
        x2 = x.reshape(batch * seq, d)

        w_in_b = w_in[l].astype(BF16)
        w_vt = _v_weight_transposed(w_in[l])
        gq = (jnp.tile(g_q[l], 2) * (DA_QK_DIM ** -0.5 * LOG2E))[None, :]
        gk = jnp.tile(g_k[l], 2)[None, :]
        qa, ka, vt, qkr, vr, ga, gbs = _in_proj(
            x2, mod3, g_norm1[l][None, :], w_in_b, w_vt, gq, gk, tabs, batch, seq)

        g_col = (g_da_out[l] * (1.0 - lambda_init))[:, None]
        oa = _diff_attn(qa.reshape(batch, seq, d), ka.reshape(batch, seq, d), vt,
                        lam, g_col, batch, seq)
        orr = _retention(qkr.reshape(batch, seq, d), vr.reshape(batch, seq, d),
                         g_ret_out[l][None, :], log_gamma, batch, seq)

        out = _out_mlp(x2, oa.reshape(batch * seq, d), orr.reshape(batch * seq, d),
                       ga, gbs, mod3, g_norm2[l][None, :],
                       w_out[l].astype(BF16), w_up[l].astype(BF16),
                       w_down[l].astype(BF16), seq)
        x = out.reshape(batch, seq, d)
    return x
```

```python
import functools
import math

import numpy as np
import jax
import jax.numpy as jnp
from jax import lax
from jax.experimental import pallas as pl
from jax.experimental.pallas import tpu as pltpu

D_MODEL = 1024
CHUNK = 64
ROPE_THETA = 10000.0
EPS = 1e-6
DA_HEADS = 8
DA_V_DIM = D_MODEL // DA_HEADS
DA_QK_DIM = DA_V_DIM // 2
RET_HEADS = 4
RET_V_DIM = D_MODEL // RET_HEADS
RET_K_DIM = RET_V_DIM // 2
D_FF = 4 * D_MODEL
N_MOD = 6
IN_COLS = 8 * D_MODEL

C_QA, C_KA, C_VA, C_QR, C_KR, C_VR, C_GR, C_GA, C_GB = (
    0, 1024, 2048, 3072, 3584, 4096, 5120, 6144, 7168)

LANES = 128
V7X_VMEM_BYTES = 64 * 1024 * 1024
V7X_VMEM_LIMIT = V7X_VMEM_BYTES - 8 * 1024 * 1024

MOD_COL_BLOCK = 1536
ROPE_HALF = DA_QK_DIM // 2

TM_PROJ = 512
TQ = 512
TK = 512
ATTN_HEADS_PER_STEP = 4
BOUND_MARGIN = 1.03
ROW_SUM_FLOOR = 2.0 ** -60
LOG2E = math.log2(math.e)
RET_CHUNK = 256
RET_CHUNKS_PER_STEP = 8
TM_MLP = 512

F32 = jnp.float32
BF16 = jnp.bfloat16
NT_DIMS = (((1,), (1,)), ((), ()))
TN_DIMS = (((0,), (0,)), ((), ()))


def _resident(shape):
    return pl.BlockSpec(shape, lambda *_: (0,) * len(shape),
                        pipeline_mode=pl.Buffered(1))


def _mod_kernel(c_ref, w_ref, b_ref, lq1_ref, lk1_ref, lq2_ref, lk2_ref,
                mod_ref, lam_ref, *, lambda_init):
    c = c_ref[...]
    sc = (c * jax.nn.sigmoid(c)).astype(BF16)
    mod_ref[...] = jnp.dot(sc, w_ref[...].astype(BF16),
                           preferred_element_type=F32) + b_ref[...]
    s1 = jnp.sum(lq1_ref[...] * lk1_ref[...], axis=-1, keepdims=True)
    s2 = jnp.sum(lq2_ref[...] * lk2_ref[...], axis=-1, keepdims=True)
    lam_ref[...] = jnp.exp(s1) - jnp.exp(s2) + lambda_init


def _modulation(c, w_ada, b_ada, lq1, lk1, lq2, lk2, lambda_init):
    b, d = c.shape
    n = w_ada.shape[1]
    bn = MOD_COL_BLOCK
    vec = pl.BlockSpec((1, DA_QK_DIM), lambda j: (0, 0))
    return pl.pallas_call(
        functools.partial(_mod_kernel, lambda_init=lambda_init),
        grid=(n // bn,),
        in_specs=[pl.BlockSpec((b, d), lambda j: (0, 0)),
                  pl.BlockSpec((d, bn), lambda j: (0, j)),
                  pl.BlockSpec((1, bn), lambda j: (0, j)),
                  vec, vec, vec, vec],
        out_specs=[pl.BlockSpec((b, bn), lambda j: (0, j)),
                   pl.BlockSpec((1, 1), lambda j: (0, 0))],
        out_shape=[jax.ShapeDtypeStruct((b, n), F32),
                   jax.ShapeDtypeStruct((1, 1), F32)],
        compiler_params=pltpu.CompilerParams(
            dimension_semantics=("arbitrary",),
            vmem_limit_bytes=V7X_VMEM_LIMIT),
        name="modulation",
    )(c, w_ada, b_ada, lq1, lk1, lq2, lk2)


def _transpose_cast_kernel(w_ref, o_ref):
    o_ref[...] = w_ref[...].T.astype(BF16)


def _v_weight_transposed(w_in_l):
    d = w_in_l.shape[0]
    return pl.pallas_call(
        _transpose_cast_kernel,
        grid=(1,),
        in_specs=[pl.BlockSpec((d, d), lambda i: (0, C_VA // d))],
        out_specs=pl.BlockSpec((d, d), lambda i: (0, 0)),
        out_shape=jax.ShapeDtypeStruct((d, d), BF16),
        compiler_params=pltpu.CompilerParams(
            dimension_semantics=("arbitrary",),
            vmem_limit_bytes=V7X_VMEM_LIMIT),
        name="v_weight_t",
    )(w_in_l)


def _in_proj_kernel(x_ref, mod_ref, g1_ref, w_ref, wvt_ref, gq_ref, gk_ref,
                    cda_ref, sda_a_ref, sda_b_ref, cr_ref, sr_ref,
                    qa_ref, ka_ref, vt_ref, qkr_ref, vr_ref, ga_ref, gbs_ref):
    tm = x_ref.shape[0]
    x = x_ref[...]
    ms = jnp.mean(x * x, axis=-1, keepdims=True)
    h = x * lax.rsqrt(ms + EPS) * g1_ref[...]
    mod = mod_ref[0]
    hm = (h * (1.0 + mod[1:2, :]) + mod[0:1, :]).astype(BF16)

    def proj(col, width):
        return jnp.dot(hm, w_ref[:, col:col + width], preferred_element_type=F32)

    cda, sda_a, sda_b = cda_ref[...], sda_a_ref[...], sda_b_ref[...]
    first_map = lax.broadcasted_iota(jnp.int32, (tm, LANES), 1) < DA_QK_DIM

    def qk_norm_rope(acc, gain, out_ref):
        for j in range(D_MODEL // LANES):
            a = acc[:, j * LANES:(j + 1) * LANES]
            a2 = a * a
            ms1 = jnp.sum(jnp.where(first_map, a2, 0.0), axis=-1, keepdims=True)
            ms2 = jnp.sum(jnp.where(first_map, 0.0, a2), axis=-1, keepdims=True)
            rs = lax.rsqrt(jnp.where(first_map, ms1, ms2) * (1.0 / DA_QK_DIM) + EPS)
            a = a * rs * gain
            rot = (a * cda + pltpu.roll(a, LANES - ROPE_HALF, 1) * sda_a
                   + pltpu.roll(a, ROPE_HALF, 1) * sda_b)
            out_ref[:, j * LANES:(j + 1) * LANES] = rot.astype(BF16)

    qk_norm_rope(proj(C_QA, D_MODEL), gq_ref[...], qa_ref)
    qk_norm_rope(proj(C_KA, D_MODEL), gk_ref[...], ka_ref)

    vt = lax.dot_general(wvt_ref[...], hm, NT_DIMS, preferred_element_type=F32)
    vt_ref[0, 0] = vt.astype(BF16)

    qkr = proj(C_QR, D_MODEL)
    cr, sr = cr_ref[...], sr_ref[...]
    for j in range(D_MODEL // LANES):
        a = qkr[:, j * LANES:(j + 1) * LANES]
        rot = a * cr + pltpu.roll(a, LANES // 2, 1) * sr
        if j >= RET_HEADS:
            rot = rot * (RET_K_DIM ** -0.5)
        qkr_ref[:, j * LANES:(j + 1) * LANES] = rot.astype(BF16)

    vr_ref[...] = proj(C_VR, D_MODEL).astype(BF16)

    gr = proj(C_GR, D_MODEL)
    ga_ref[...] = jax.nn.sigmoid(proj(C_GA, D_MODEL)).astype(BF16)
    gbs_ref[...] = (jax.nn.sigmoid(proj(C_GB, D_MODEL))
                    * (gr * jax.nn.sigmoid(gr))).astype(BF16)


def _in_proj(x2, mod3, g1, w_in, w_vt, gq, gk, tabs, batch, seq):
    t, d = x2.shape
    tm = TM_PROJ
    tpb = seq // tm
    row = pl.BlockSpec((tm, d), lambda i: (i, 0))
    tab = pl.BlockSpec((tm, LANES), lambda i: (i % tpb, 0))
    vec = pl.BlockSpec((1, LANES), lambda i: (0, 0))
    out_row = jax.ShapeDtypeStruct((t, d), BF16)
    return pl.pallas_call(
        _in_proj_kernel,
        grid=(t // tm,),
        in_specs=[row,
                  pl.BlockSpec((1, N_MOD, d), lambda i: (i // tpb, 0, 0)),
                  pl.BlockSpec((1, d), lambda i: (0, 0)),
                  _resident((d, IN_COLS)),
                  _resident((d, d)),
                  vec, vec, tab, tab, tab, tab, tab],
        out_specs=[row, row,
                   pl.BlockSpec((1, 1, d, tm),
                                lambda i: (i // tpb, (i % tpb) // (TK // tm), 0,
                                           (i % tpb) % (TK // tm))),
                   row, row, row, row],
        out_shape=[out_row, out_row,
                   jax.ShapeDtypeStruct((batch, seq // TK, d, TK), BF16),
                   out_row, out_row, out_row, out_row],
        compiler_params=pltpu.CompilerParams(
            dimension_semantics=("arbitrary",),
            vmem_limit_bytes=V7X_VMEM_LIMIT),
        name="in_proj",
    )(x2, mod3, g1, w_in, w_vt, gq, gk, *tabs)


def _diff_attn_kernel(q_ref, k_ref, vt_ref, bias_ref, lam_ref, g_ref, o_ref,
                      acc_ref, l_ref, m_ref, bound_ref, kmax_ref):
    qi = pl.program_id(2)
    heads = range(ATTN_HEADS_PER_STEP)
    lane = lax.broadcasted_iota(jnp.int32, (TQ, LANES), 1)
    qcat = []
    for hd in heads:
        q = q_ref[0, :, hd * LANES:(hd + 1) * LANES]
        zero = jnp.zeros_like(q)
        qcat.append(jnp.concatenate([jnp.where(lane < DA_QK_DIM, q, zero),
                                     jnp.where(lane >= DA_QK_DIM, q, zero)], axis=0))

    def k_block(j, hd):
        koff = pl.multiple_of(j * TK, TK)
        return k_ref[0, pl.ds(koff, TK), hd * LANES:(hd + 1) * LANES]

    def v_block(j, hd):
        return vt_ref[0, j, hd * DA_V_DIM:(hd + 1) * DA_V_DIM, :]

    @pl.when(qi == 0)
    def _():
        gi = lax.broadcasted_iota(jnp.int32, (LANES, LANES), 0) // DA_QK_DIM
        gj = lax.broadcasted_iota(jnp.int32, (LANES, LANES), 1) // DA_QK_DIM
        group_sum = jnp.where(gi == gj, 1.0, 0.0).astype(BF16)
        for hd in heads:
            k = k_ref[0, :, hd * LANES:(hd + 1) * LANES].astype(F32)
            norm2 = jnp.dot((k * k).astype(BF16), group_sum, preferred_element_type=F32)
            kmax_ref[hd] = jnp.max(norm2, axis=0, keepdims=True)

    def set_bounds():
        for hd in heads:
            qsq = jnp.square(qcat[hd].astype(F32)).astype(BF16)
            kmax = jnp.broadcast_to(kmax_ref[hd], (16, LANES)).astype(BF16)
            ub2 = lax.dot_general(kmax, qsq, NT_DIMS, preferred_element_type=F32)
            bound_ref[hd] = jnp.sqrt(ub2[0:1, :] * BOUND_MARGIN)

    def stream(blocks, with_diagonal):
        for hd in heads:
            pv, row_sum = None, None
            for n, j in enumerate(blocks):
                s = lax.dot_general(k_block(j, hd), qcat[hd], NT_DIMS,
                                    preferred_element_type=F32)
                if with_diagonal and n == len(blocks) - 1:
                    s = s + bias_ref[...]
                e = jnp.exp2(s - bound_ref[hd])
                part_sum = jnp.sum(e, axis=0, keepdims=True)
                part = jnp.dot(v_block(j, hd), e.astype(BF16),
                               preferred_element_type=F32)
                pv = part if pv is None else pv + part
                row_sum = part_sum if row_sum is None else row_sum + part_sum
            acc_ref[hd] = pv if with_diagonal else acc_ref[hd] + pv
            l_ref[hd] = row_sum if with_diagonal else l_ref[hd] + row_sum

    @pl.when(qi % 2 == 0)
    def _():
        set_bounds()
        stream([qi], True)

    @pl.when(qi % 2 == 1)
    def _():
        set_bounds()
        stream([qi - 1, qi], True)

    def stream_body(t, carry):
        stream([2 * t, 2 * t + 1], False)
        return carry

    lax.fori_loop(0, qi // 2, stream_body, 0)

    def finalize():
        for hd in heads:
            inv_l = 1.0 / l_ref[hd]
            o = (acc_ref[hd, :, :TQ] * inv_l[:, :TQ]
                 - acc_ref[hd, :, TQ:] * (lam_ref[...] * inv_l[:, TQ:]))
            ms = jnp.mean(o * o, axis=0, keepdims=True)
            on = o * lax.rsqrt(ms + EPS) * g_ref[...]
            o_ref[0, :, hd * LANES:(hd + 1) * LANES] = on.T.astype(BF16)

    finalize()
    l_min = jnp.min(l_ref[...])

    def online_step(j, carry):
        for hd in heads:
            s = lax.dot_general(k_block(j, hd), qcat[hd], NT_DIMS,
                                preferred_element_type=F32)
            s = s + jnp.where(j == qi, bias_ref[...], 0.0)
            m_old = m_ref[hd]
            m_new = jnp.maximum(m_old, jnp.max(s, axis=0, keepdims=True))
            alpha = jnp.exp2(m_old - m_new)
            e = jnp.exp2(s - m_new)
            pv = jnp.dot(v_block(j, hd), e.astype(BF16), preferred_element_type=F32)
            acc_ref[hd] = alpha * acc_ref[hd] + pv
            l_ref[hd] = alpha * l_ref[hd] + jnp.sum(e, axis=0, keepdims=True)
            m_ref[hd] = m_new
        return carry

    @pl.when(jnp.logical_not(l_min >= ROW_SUM_FLOOR))
    def _():
        m_ref[...] = jnp.full(m_ref.shape, -jnp.inf, F32)
        acc_ref[...] = jnp.zeros(acc_ref.shape, F32)
        l_ref[...] = jnp.zeros(l_ref.shape, F32)
        lax.fori_loop(0, qi + 1, online_step, 0)
        finalize()


def _diag_bias():
    kc = np.arange(TK)[:, None] // CHUNK
    qc = (np.arange(2 * TQ)[None, :] % TQ) // CHUNK
    return jnp.asarray(np.where(kc <= qc, 0.0, -np.inf).astype(np.float32))


def _diff_attn(qa, ka, vt, lam, g_col, batch, seq):
    hps = ATTN_HEADS_PER_STEP
    nq = seq // TQ
    return pl.pallas_call(
        _diff_attn_kernel,
        grid=(batch, DA_HEADS // hps, nq),
        in_specs=[pl.BlockSpec((1, TQ, hps * LANES), lambda b, h, i: (b, i, h)),
                  pl.BlockSpec((1, seq, hps * LANES), lambda b, h, i: (b, 0, h)),
                  pl.BlockSpec((1, seq // TK, hps * DA_V_DIM, TK), lambda b, h, i: (b, 0, h, 0)),
                  _resident((TK, 2 * TQ)),
                  pl.BlockSpec((1, 1), lambda b, h, i: (0, 0)),
                  pl.BlockSpec((DA_V_DIM, 1), lambda b, h, i: (0, 0))],
        out_specs=pl.BlockSpec((1, TQ, hps * LANES), lambda b, h, i: (b, i, h)),
        out_shape=jax.ShapeDtypeStruct((batch, seq, D_MODEL), BF16),
        scratch_shapes=[pltpu.VMEM((hps, DA_V_DIM, 2 * TQ), F32),
                        pltpu.VMEM((hps, 1, 2 * TQ), F32),
                        pltpu.VMEM((hps, 1, 2 * TQ), F32),
                        pltpu.VMEM((hps, 1, 2 * TQ), F32),
                        pltpu.VMEM((hps, 1, LANES), F32)],
        compiler_params=pltpu.CompilerParams(
            dimension_semantics=("arbitrary", "arbitrary", "arbitrary"),
            vmem_limit_bytes=V7X_VMEM_LIMIT),
        name="diff_attn",
    )(qa, ka, vt, _diag_bias(), lam, g_col)


def _retention_kernel(qk_ref, v_ref, g_ref, intra_ref, inner_ref, kvd_ref, o_ref,
                      state_ref, *, chunk_decay):
    @pl.when(pl.program_id(1) == 0)
    def _():
        state_ref[...] = jnp.zeros(state_ref.shape, F32)

    cr = intra_ref.shape[1]
    for ci in range(RET_CHUNKS_PER_STEP):
        rows = slice(ci * cr, (ci + 1) * cr)
        for hd in range(RET_HEADS):
            q = qk_ref[0, rows, hd * RET_K_DIM:(hd + 1) * RET_K_DIM]
            k = qk_ref[0, rows, (RET_HEADS + hd) * RET_K_DIM:(RET_HEADS + hd + 1) * RET_K_DIM]
            v = v_ref[0, rows, hd * RET_V_DIM:(hd + 1) * RET_V_DIM]
            sc = lax.dot_general(q, k, NT_DIMS, preferred_element_type=F32) * intra_ref[hd]
            y = jnp.dot(sc.astype(BF16), v, preferred_element_type=F32)
            st = state_ref[hd]
            cross = jnp.dot(q, st.astype(BF16), preferred_element_type=F32)
            y = y + cross * inner_ref[hd]
            kd = (k.astype(F32) * kvd_ref[hd]).astype(BF16)
            upd = lax.dot_general(kd, v, TN_DIMS, preferred_element_type=F32)
            state_ref[hd] = st * chunk_decay[hd] + upd
            ms = jnp.mean(y * y, axis=-1, keepdims=True)
            o_ref[0, rows, hd * RET_V_DIM:(hd + 1) * RET_V_DIM] = (
                y * lax.rsqrt(ms + EPS) * g_ref[...]).astype(BF16)


def _retention_tables(log_gamma, cr):
    n = np.arange(cr, dtype=np.float32)
    lg = log_gamma.astype(np.float32)[:, None, None]
    diff = n[:, None] - n[None, :]
    intra = np.where(diff >= 0, np.exp(np.maximum(diff, 0.0) * lg), 0.0).astype(np.float32)
    inner = np.exp((n + 1.0)[None, :, None] * lg).astype(np.float32)
    kvd = np.exp((cr - 1.0 - n)[None, :, None] * lg).astype(np.float32)
    inner = np.broadcast_to(inner, (RET_HEADS, cr, RET_V_DIM))
    kvd = np.broadcast_to(kvd, (RET_HEADS, cr, RET_K_DIM))
    chunk = tuple(float(np.exp(np.float32(cr) * g)) for g in log_gamma.astype(np.float32))
    return jnp.asarray(intra), jnp.asarray(inner), jnp.asarray(kvd), chunk


def _retention(qkr, vr, g_ret, log_gamma, batch, seq):
    cr = RET_CHUNK
    intra, inner, kvd, chunk_decay = _retention_tables(log_gamma, cr)
    step_rows = cr * RET_CHUNKS_PER_STEP
    rows = pl.BlockSpec((1, step_rows, D_MODEL), lambda b, c: (b, c, 0))
    return pl.pallas_call(
        functools.partial(_retention_kernel, chunk_decay=chunk_decay),
        grid=(batch, seq // step_rows),
        in_specs=[rows, rows,
                  pl.BlockSpec((1, RET_V_DIM), lambda b, c: (0, 0)),
                  _resident((RET_HEADS, cr, cr)),
                  _resident((RET_HEADS, cr, RET_V_DIM)),
                  _resident((RET_HEADS, cr, RET_K_DIM))],
        out_specs=rows,
        out_shape=jax.ShapeDtypeStruct((batch, seq, D_MODEL), BF16),
        scratch_shapes=[pltpu.VMEM((RET_HEADS, RET_K_DIM, RET_V_DIM), F32)],
        compiler_params=pltpu.CompilerParams(
            dimension_semantics=("arbitrary", "arbitrary"),
            vmem_limit_bytes=V7X_VMEM_LIMIT),
        name="retention",
    )(qkr, vr, g_ret, intra, inner, kvd)


def _out_mlp_kernel(x_ref, oa_ref, or_ref, ga_ref, gbs_ref, mod_ref, g2_ref,
                    wo_ref, wu_ref, wd_ref, out_ref):
    mod = mod_ref[0]
    u = (ga_ref[...].astype(F32) * oa_ref[...].astype(F32)
         + gbs_ref[...].astype(F32) * or_ref[...].astype(F32)).astype(BF16)
    y = jnp.dot(u, wo_ref[...], preferred_element_type=F32)
    x1 = x_ref[...] + mod[2:3, :] * y
    ms = jnp.mean(x1 * x1, axis=-1, keepdims=True)
    hn = x1 * lax.rsqrt(ms + EPS) * g2_ref[...]
    hff = (hn * (1.0 + mod[4:5, :]) + mod[3:4, :]).astype(BF16)
    up = jnp.dot(hff, wu_ref[...], preferred_element_type=F32)
    act = jnp.square(jnp.maximum(up, 0.0)).astype(BF16)
    ff = jnp.dot(act, wd_ref[...], preferred_element_type=F32)
    out_ref[...] = x1 + mod[5:6, :] * ff


def _out_mlp(x2, oa, orr, ga, gbs, mod3, g2, w_out, w_up, w_down, seq):
    t, d = x2.shape
    tm = TM_MLP
    tpb = seq // tm
    row = pl.BlockSpec((tm, d), lambda i: (i, 0))
    return pl.pallas_call(
        _out_mlp_kernel,
        grid=(t // tm,),
        in_specs=[row, row, row, row, row,
                  pl.BlockSpec((1, N_MOD, d), lambda i: (i // tpb, 0, 0)),
                  pl.BlockSpec((1, d), lambda i: (0, 0)),
                  _resident((d, d)), _resident((d, D_FF)), _resident((D_FF, d))],
        out_specs=row,
        out_shape=jax.ShapeDtypeStruct((t, d), F32),
        compiler_params=pltpu.CompilerParams(
            dimension_semantics=("arbitrary",),
            vmem_limit_bytes=V7X_VMEM_LIMIT),
        name="out_mlp",
    )(x2, oa, orr, ga, gbs, mod3, g2, w_out, w_up, w_down)


def _rope_tables(seq):
    pos = jnp.arange(seq, dtype=F32)[:, None]
    da_inv = ROPE_THETA ** (-jnp.arange(0, DA_QK_DIM, 2, dtype=F32) / DA_QK_DIM)
    ang = pos * da_inv[None, :]
    cos, sin = jnp.cos(ang), jnp.sin(ang)
    zero = jnp.zeros_like(sin)
    cda = jnp.tile(cos, (1, 4))
    sda_a = jnp.tile(jnp.concatenate([-sin, zero], axis=1), (1, 2))
    sda_b = jnp.tile(jnp.concatenate([zero, sin], axis=1), (1, 2))
    ret_inv = 1.0 / (ROPE_THETA ** jnp.linspace(0.0, 1.0, RET_K_DIM // 2, dtype=F32))
    ang_r = pos * ret_inv[None, :]
    cos_r, sin_r = jnp.cos(ang_r), jnp.sin(ang_r)
    cr = jnp.concatenate([cos_r, cos_r], axis=1)
    sr = jnp.concatenate([-sin_r, sin_r], axis=1)
    return cda, sda_a, sda_b, cr, sr


def kernel(x, c, w_ada, b_ada, g_norm1, w_in, g_q, g_k, lambda_q1, lambda_k1,
           lambda_q2, lambda_k2, g_da_out, g_ret_out, w_out, g_norm2, w_up, w_down):
    batch, seq, d = x.shape
    depth = w_ada.shape[0]
    assert d == D_MODEL and w_in.shape[1:] == (D_MODEL, IN_COLS), (x.shape, w_in.shape)
    for tile in (TM_PROJ, TM_MLP, TQ, TK, RET_CHUNK * RET_CHUNKS_PER_STEP):
        assert seq % tile == 0, (seq, tile)
    assert TQ == TK and TK % TM_PROJ == 0 and TQ % CHUNK == 0
    log_gamma = np.log(1.0 - 2.0 ** (-5.0 - np.arange(RET_HEADS))).astype(np.float32)
    tabs = _rope_tables(seq)

    for l in range(depth):
        lambda_init = 0.8 - 0.6 * math.exp(-0.3 * l)
        mod, lam = _modulation(c, w_ada[l], b_ada[l][None, :],
                               lambda_q1[l][None, :], lambda_k1[l][None, :],
                               lambda_q2[l][None, :], lambda_k2[l][None, :],
                               lambda_init)
        mod3 = mod.reshape(batch, N_MOD, d)
        x2 = x.reshape(batch * seq, d)

        w_in_b = w_in[l].astype(BF16)
        w_vt = _v_weight_transposed(w_in[l])
        gq = (jnp.tile(g_q[l], 2) * (DA_QK_DIM ** -0.5 * LOG2E))[None, :]
        gk = jnp.tile(g_k[l], 2)[None, :]
        qa, ka, vt, qkr, vr, ga, gbs = _in_proj(
            x2, mod3, g_norm1[l][None, :], w_in_b, w_vt, gq, gk, tabs, batch, seq)

        g_col = (g_da_out[l] * (1.0 - lambda_init))[:, None]
        oa = _diff_attn(qa.reshape(batch, seq, d), ka.reshape(batch, seq, d), vt,
                        lam, g_col, batch, seq)
        orr = _retention(qkr.reshape(batch, seq, d), vr.reshape(batch, seq, d),
                         g_ret_out[l][None, :], log_gamma, batch, seq)

        out = _out_mlp(x2, oa.reshape(batch * seq, d), orr.reshape(batch * seq, d),
                       ga, gbs, mod3, g_norm2[l][None, :],
                       w_out[l].astype(BF16), w_up[l].astype(BF16),
                       w_down[l].astype(BF16), seq)
        x = out.reshape(batch, seq, d)
    return x
```

```python
import functools
import math

import numpy as np
import jax
import jax.numpy as jnp
from jax import lax
from jax.experimental import pallas as pl
from jax.experimental.pallas import tpu as pltpu

D_MODEL = 1024
CHUNK = 64
ROPE_THETA = 10000.0
EPS = 1e-6
DA_HEADS = 8
DA_V_DIM = D_MODEL // DA_HEADS
DA_QK_DIM = DA_V_DIM // 2
RET_HEADS = 4
RET_V_DIM = D_MODEL // RET_HEADS
RET_K_DIM = RET_V_DIM // 2
D_FF = 4 * D_MODEL
N_MOD = 6
IN_COLS = 8 * D_MODEL

C_QA, C_KA, C_VA, C_QR, C_KR, C_VR, C_GR, C_GA, C_GB = (
    0, 1024, 2048, 3072, 3584, 4096, 5120, 6144, 7168)

LANES = 128
V7X_VMEM_BYTES = 64 * 1024 * 1024
V7X_VMEM_LIMIT = V7X_VMEM_BYTES - 8 * 1024 * 1024

MOD_COL_BLOCK = 1536
ROPE_HALF = DA_QK_DIM // 2

TM_PROJ = 512
TQ = 512
TK = 512
ATTN_HEADS_PER_STEP = 4
BOUND_MARGIN = 1.03
ROW_SUM_FLOOR = 2.0 ** -60
LOG2E = math.log2(math.e)
RET_CHUNK = 256
RET_CHUNKS_PER_STEP = 8
TM_MLP = 512

F32 = jnp.float32
BF16 = jnp.bfloat16
NT_DIMS = (((1,), (1,)), ((), ()))
TN_DIMS = (((0,), (0,)), ((), ()))


def _resident(shape):
    return pl.BlockSpec(shape, lambda *_: (0,) * len(shape),
                        pipeline_mode=pl.Buffered(1))


def _mod_kernel(c_ref, w_ref, b_ref, lq1_ref, lk1_ref, lq2_ref, lk2_ref,
                mod_ref, lam_ref, *, lambda_init):
    c = c_ref[...]
    sc = (c * jax.nn.sigmoid(c)).astype(BF16)
    mod_ref[...] = jnp.dot(sc, w_ref[...].astype(BF16),
                           preferred_element_type=F32) + b_ref[...]
    s1 = jnp.sum(lq1_ref[...] * lk1_ref[...], axis=-1, keepdims=True)
    s2 = jnp.sum(lq2_ref[...] * lk2_ref[...], axis=-1, keepdims=True)
    lam_ref[...] = jnp.exp(s1) - jnp.exp(s2) + lambda_init


def _modulation(c, w_ada, b_ada, lq1, lk1, lq2, lk2, lambda_init):
    b, d = c.shape
    n = w_ada.shape[1]
    bn = MOD_COL_BLOCK
    vec = pl.BlockSpec((1, DA_QK_DIM), lambda j: (0, 0))
    return pl.pallas_call(
        functools.partial(_mod_kernel, lambda_init=lambda_init),
        grid=(n // bn,),
        in_specs=[pl.BlockSpec((b, d), lambda j: (0, 0)),
                  pl.BlockSpec((d, bn), lambda j: (0, j)),
                  pl.BlockSpec((1, bn), lambda j: (0, j)),
                  vec, vec, vec, vec],
        out_specs=[pl.BlockSpec((b, bn), lambda j: (0, j)),
                   pl.BlockSpec((1, 1), lambda j: (0, 0))],
        out_shape=[jax.ShapeDtypeStruct((b, n), F32),
                   jax.ShapeDtypeStruct((1, 1), F32)],
        compiler_params=pltpu.CompilerParams(
            dimension_semantics=("arbitrary",),
            vmem_limit_bytes=V7X_VMEM_LIMIT),
        name="modulation",
    )(c, w_ada, b_ada, lq1, lk1, lq2, lk2)


def _transpose_cast_kernel(w_ref, o_ref):
    o_ref[...] = w_ref[...].T.astype(BF16)


def _v_weight_transposed(w_in_l):
    d = w_in_l.shape[0]
    return pl.pallas_call(
        _transpose_cast_kernel,
        grid=(1,),
        in_specs=[pl.BlockSpec((d, d), lambda i: (0, C_VA // d))],
        out_specs=pl.BlockSpec((d, d), lambda i: (0, 0)),
        out_shape=jax.ShapeDtypeStruct((d, d), BF16),
        compiler_params=pltpu.CompilerParams(
            dimension_semantics=("arbitrary",),
            vmem_limit_bytes=V7X_VMEM_LIMIT),
        name="v_weight_t",
    )(w_in_l)


def _in_proj_kernel(x_ref, mod_ref, g1_ref, w_ref, wvt_ref, gq_ref, gk_ref,
                    cda_ref, sda_a_ref, sda_b_ref, cr_ref, sr_ref,
                    qa_ref, ka_ref, vt_ref, qkr_ref, vr_ref, ga_ref, gbs_ref):
    tm = x_ref.shape[0]
    x = x_ref[...]
    ms = jnp.mean(x * x, axis=-1, keepdims=True)
    h = x * lax.rsqrt(ms + EPS) * g1_ref[...]
    mod = mod_ref[0]
    hm = (h * (1.0 + mod[1:2, :]) + mod[0:1, :]).astype(BF16)

    def proj(col, width):
        return jnp.dot(hm, w_ref[:, col:col + width], preferred_element_type=F32)

    cda, sda_a, sda_b = cda_ref[...], sda_a_ref[...], sda_b_ref[...]
    first_map = lax.broadcasted_iota(jnp.int32, (tm, LANES), 1) < DA_QK_DIM

    def qk_norm_rope(acc, gain, out_ref):
        for j in range(D_MODEL // LANES):
            a = acc[:, j * LANES:(j + 1) * LANES]
            a2 = a * a
            ms1 = jnp.sum(jnp.where(first_map, a2, 0.0), axis=-1, keepdims=True)
            ms2 = jnp.sum(jnp.where(first_map, 0.0, a2), axis=-1, keepdims=True)
            rs = lax.rsqrt(jnp.where(first_map, ms1, ms2) * (1.0 / DA_QK_DIM) + EPS)
            a = a * rs * gain
            rot = (a * cda + pltpu.roll(a, LANES - ROPE_HALF, 1) * sda_a
                   + pltpu.roll(a, ROPE_HALF, 1) * sda_b)
            out_ref[:, j * LANES:(j + 1) * LANES] = rot.astype(BF16)

    qk_norm_rope(proj(C_QA, D_MODEL), gq_ref[...], qa_ref)
    qk_norm_rope(proj(C_KA, D_MODEL), gk_ref[...], ka_ref)

    vt = lax.dot_general(wvt_ref[...], hm, NT_DIMS, preferred_element_type=F32)
    vt_ref[0, 0] = vt.astype(BF16)

    qkr = proj(C_QR, D_MODEL)
    cr, sr = cr_ref[...], sr_ref[...]
    for j in range(D_MODEL // LANES):
        a = qkr[:, j * LANES:(j + 1) * LANES]
        rot = a * cr + pltpu.roll(a, LANES // 2, 1) * sr
        if j >= RET_HEADS:
            rot = rot * (RET_K_DIM ** -0.5)
        qkr_ref[:, j * LANES:(j + 1) * LANES] = rot.astype(BF16)

    vr_ref[...] = proj(C_VR, D_MODEL).astype(BF16)

    gr = proj(C_GR, D_MODEL)
    ga_ref[...] = jax.nn.sigmoid(proj(C_GA, D_MODEL)).astype(BF16)
    gbs_ref[...] = (jax.nn.sigmoid(proj(C_GB, D_MODEL))
                    * (gr * jax.nn.sigmoid(gr))).astype(BF16)


def _in_proj(x2, mod3, g1, w_in, w_vt, gq, gk, tabs, batch, seq):
    t, d = x2.shape
    tm = TM_PROJ
    tpb = seq // tm
    row = pl.BlockSpec((tm, d), lambda i: (i, 0))
    tab = pl.BlockSpec((tm, LANES), lambda i: (i % tpb, 0))
    vec = pl.BlockSpec((1, LANES), lambda i: (0, 0))
    out_row = jax.ShapeDtypeStruct((t, d), BF16)
    return pl.pallas_call(
        _in_proj_kernel,
        grid=(t // tm,),
        in_specs=[row,
                  pl.BlockSpec((1, N_MOD, d), lambda i: (i // tpb, 0, 0)),
                  pl.BlockSpec((1, d), lambda i: (0, 0)),
                  _resident((d, IN_COLS)),
                  _resident((d, d)),
                  vec, vec, tab, tab, tab, tab, tab],
        out_specs=[row, row,
                   pl.BlockSpec((1, 1, d, tm),
                                lambda i: (i // tpb, (i % tpb) // (TK // tm), 0,
                                           (i % tpb) % (TK // tm))),
                   row, row, row, row],
        out_shape=[out_row, out_row,
                   jax.ShapeDtypeStruct((batch, seq // TK, d, TK), BF16),
                   out_row, out_row, out_row, out_row],
        compiler_params=pltpu.CompilerParams(
            dimension_semantics=("arbitrary",),
            vmem_limit_bytes=V7X_VMEM_LIMIT),
        name="in_proj",
    )(x2, mod3, g1, w_in, w_vt, gq, gk, *tabs)


def _diff_attn_kernel(q_ref, k_ref, vt_ref, bias_ref, lam_ref, g_ref, o_ref,
                      qcat_ref, acc_ref, l_ref, m_ref, bound_ref, kmax_ref, early_ref, *, nq):
    step = pl.program_id(2)
    slot = step % 2
    working = step < nq
    heads = range(ATTN_HEADS_PER_STEP)

    def k_block(j, hd):
        koff = pl.multiple_of(j * TK, TK)
        return k_ref[0, pl.ds(koff, TK), hd * LANES:(hd + 1) * LANES]

    def v_block(j, hd):
        return vt_ref[0, j, hd * DA_V_DIM:(hd + 1) * DA_V_DIM, :]

    def set_key_norms():
        gi = lax.broadcasted_iota(jnp.int32, (LANES, LANES), 0) // DA_QK_DIM
        gj = lax.broadcasted_iota(jnp.int32, (LANES, LANES), 1) // DA_QK_DIM
        group_sum = jnp.where(gi == gj, 1.0, 0.0).astype(BF16)
        for hd in heads:
            k = k_ref[0, :, hd * LANES:(hd + 1) * LANES].astype(F32)
            norm2 = jnp.dot((k * k).astype(BF16), group_sum, preferred_element_type=F32)
            kmax_ref[hd] = jnp.max(norm2, axis=0, keepdims=True)

    def set_queries():
        lane = lax.broadcasted_iota(jnp.int32, (TQ, LANES), 1)
        for hd in heads:
            q = q_ref[0, :, hd * LANES:(hd + 1) * LANES]
            zero = jnp.zeros_like(q)
            qcat = jnp.concatenate([jnp.where(lane < DA_QK_DIM, q, zero),
                                    jnp.where(lane >= DA_QK_DIM, q, zero)], axis=0)
            qcat_ref[hd] = qcat
            qsq = jnp.square(qcat.astype(F32)).astype(BF16)
            kmax = jnp.broadcast_to(kmax_ref[hd], (16, LANES)).astype(BF16)
            ub2 = lax.dot_general(kmax, qsq, NT_DIMS, preferred_element_type=F32)
            bound_ref[hd] = jnp.sqrt(ub2[0:1, :] * BOUND_MARGIN)

    def stream(blocks, with_diagonal, into):
        for hd in heads:
            pv, row_sum = None, None
            for n, j in enumerate(blocks):
                s = lax.dot_general(k_block(j, hd), qcat_ref[hd], NT_DIMS,
                                    preferred_element_type=F32)
                if with_diagonal and n == len(blocks) - 1:
                    s = s + bias_ref[...]
                e = jnp.exp2(s - bound_ref[hd])
                part_sum = jnp.sum(e, axis=0, keepdims=True)
                part = jnp.dot(v_block(j, hd), e.astype(BF16),
                               preferred_element_type=F32)
                pv = part if pv is None else pv + part
                row_sum = part_sum if row_sum is None else row_sum + part_sum
            acc_ref[into, hd] = pv if with_diagonal else acc_ref[into, hd] + pv
            l_ref[into, hd] = row_sum if with_diagonal else l_ref[into, hd] + row_sum
            if with_diagonal:
                first_min = jnp.min(row_sum)
                early_ref[0] = first_min if hd == 0 else jnp.minimum(early_ref[0], first_min)

    def write_out(src):
        for hd in heads:
            inv_l = 1.0 / l_ref[src, hd]
            o = (acc_ref[src, hd, :, :TQ] * inv_l[:, :TQ]
                 - acc_ref[src, hd, :, TQ:] * (lam_ref[...] * inv_l[:, TQ:]))
            ms = jnp.mean(o * o, axis=0, keepdims=True)
            on = o * lax.rsqrt(ms + EPS) * g_ref[...]
            o_ref[0, :, hd * LANES:(hd + 1) * LANES] = on.T.astype(BF16)

    @pl.when(step == 0)
    def _():
        set_key_norms()
        set_queries()
        stream([0], True, 0)

    @pl.when(jnp.logical_and(step % 2 == 1, working))
    def _():
        set_queries()
        stream([step - 1, step], True, 1)
        write_out(0)

    @pl.when(jnp.logical_and(jnp.logical_and(step % 2 == 0, step > 0), working))
    def _():
        set_queries()
        stream([step], True, 0)
        write_out(1)

    @pl.when(step == nq)
    def _():
        write_out((nq - 1) % 2)

    def stream_body(t, carry):
        stream([2 * t, 2 * t + 1], False, slot)
        return carry

    lax.fori_loop(0, jnp.where(working, step // 2, 0), stream_body, 0)

    def online_step(j, carry):
        for hd in heads:
            s = lax.dot_general(k_block(j, hd), qcat_ref[hd], NT_DIMS,
                                preferred_element_type=F32)
            s = s + jnp.where(j == step, bias_ref[...], 0.0)
            m_old = m_ref[hd]
            m_new = jnp.maximum(m_old, jnp.max(s, axis=0, keepdims=True))
            alpha = jnp.exp2(m_old - m_new)
            e = jnp.exp2(s - m_new)
            pv = jnp.dot(v_block(j, hd), e.astype(BF16), preferred_element_type=F32)
            acc_ref[slot, hd] = alpha * acc_ref[slot, hd] + pv
            l_ref[slot, hd] = alpha * l_ref[slot, hd] + jnp.sum(e, axis=0, keepdims=True)
            m_ref[hd] = m_new
        return carry

    @pl.when(jnp.logical_and(working, jnp.logical_not(early_ref[0] >= ROW_SUM_FLOOR)))
    def _():
        @pl.when(jnp.logical_not(jnp.min(l_ref[slot]) >= ROW_SUM_FLOOR))
        def _():
            m_ref[...] = jnp.full(m_ref.shape, -jnp.inf, F32)
            acc_ref[slot] = jnp.zeros(acc_ref.shape[1:], F32)
            l_ref[slot] = jnp.zeros(l_ref.shape[1:], F32)
            lax.fori_loop(0, step + 1, online_step, 0)


def _diag_bias():
    kc = np.arange(TK)[:, None] // CHUNK
    qc = (np.arange(2 * TQ)[None, :] % TQ) // CHUNK
    return jnp.asarray(np.where(kc <= qc, 0.0, -np.inf).astype(np.float32))


def _diff_attn(qa, ka, vt, lam, g_col, batch, seq):
    hps = ATTN_HEADS_PER_STEP
    nq = seq // TQ
    return pl.pallas_call(
        functools.partial(_diff_attn_kernel, nq=nq),
        grid=(batch, DA_HEADS // hps, nq + 1),
        in_specs=[pl.BlockSpec((1, TQ, hps * LANES),
                               lambda b, h, i: (b, jnp.minimum(i, nq - 1), h)),
                  pl.BlockSpec((1, seq, hps * LANES), lambda b, h, i: (b, 0, h)),
                  pl.BlockSpec((1, seq // TK, hps * DA_V_DIM, TK), lambda b, h, i: (b, 0, h, 0)),
                  _resident((TK, 2 * TQ)),
                  pl.BlockSpec((1, 1), lambda b, h, i: (0, 0)),
                  pl.BlockSpec((DA_V_DIM, 1), lambda b, h, i: (0, 0))],
        out_specs=pl.BlockSpec((1, TQ, hps * LANES),
                               lambda b, h, i: (b, jnp.maximum(i - 1, 0), h)),
        out_shape=jax.ShapeDtypeStruct((batch, seq, D_MODEL), BF16),
        scratch_shapes=[pltpu.VMEM((hps, 2 * TQ, LANES), BF16),
                        pltpu.VMEM((2, hps, DA_V_DIM, 2 * TQ), F32),
                        pltpu.VMEM((2, hps, 1, 2 * TQ), F32),
                        pltpu.VMEM((hps, 1, 2 * TQ), F32),
                        pltpu.VMEM((hps, 1, 2 * TQ), F32),
                        pltpu.VMEM((hps, 1, LANES), F32),
                        pltpu.SMEM((1,), F32)],
        compiler_params=pltpu.CompilerParams(
            dimension_semantics=("arbitrary", "arbitrary", "arbitrary"),
            vmem_limit_bytes=V7X_VMEM_LIMIT),
        name="diff_attn",
    )(qa, ka, vt, _diag_bias(), lam, g_col)


def _retention_kernel(qk_ref, v_ref, g_ref, intra_ref, inner_ref, kvd_ref, o_ref,
                      state_ref, *, chunk_decay):
    @pl.when(pl.program_id(1) == 0)
    def _():
        state_ref[...] = jnp.zeros(state_ref.shape, F32)

    cr = intra_ref.shape[1]
    for ci in range(RET_CHUNKS_PER_STEP):
        rows = slice(ci * cr, (ci + 1) * cr)
        for hd in range(RET_HEADS):
            q = qk_ref[0, rows, hd * RET_K_DIM:(hd + 1) * RET_K_DIM]
            k = qk_ref[0, rows, (RET_HEADS + hd) * RET_K_DIM:(RET_HEADS + hd + 1) * RET_K_DIM]
            v = v_ref[0, rows, hd * RET_V_DIM:(hd + 1) * RET_V_DIM]
            sc = lax.dot_general(q, k, NT_DIMS, preferred_element_type=F32) * intra_ref[hd]
            y = jnp.dot(sc.astype(BF16), v, preferred_element_type=F32)
            st = state_ref[hd]
            cross = jnp.dot(q, st.astype(BF16), preferred_element_type=F32)
            y = y + cross * inner_ref[hd]
            kd = (k.astype(F32) * kvd_ref[hd]).astype(BF16)
            upd = lax.dot_general(kd, v, TN_DIMS, preferred_element_type=F32)
            state_ref[hd] = st * chunk_decay[hd] + upd
            ms = jnp.mean(y * y, axis=-1, keepdims=True)
            o_ref[0, rows, hd * RET_V_DIM:(hd + 1) * RET_V_DIM] = (
                y * lax.rsqrt(ms + EPS) * g_ref[...]).astype(BF16)


def _retention_tables(log_gamma, cr):
    n = np.arange(cr, dtype=np.float32)
    lg = log_gamma.astype(np.float32)[:, None, None]
    diff = n[:, None] - n[None, :]
    intra = np.where(diff >= 0, np.exp(np.maximum(diff, 0.0) * lg), 0.0).astype(np.float32)
    inner = np.exp((n + 1.0)[None, :, None] * lg).astype(np.float32)
    kvd = np.exp((cr - 1.0 - n)[None, :, None] * lg).astype(np.float32)
    inner = np.broadcast_to(inner, (RET_HEADS, cr, RET_V_DIM))
    kvd = np.broadcast_to(kvd, (RET_HEADS, cr, RET_K_DIM))
    chunk = tuple(float(np.exp(np.float32(cr) * g)) for g in log_gamma.astype(np.float32))
    return jnp.asarray(intra), jnp.asarray(inner), jnp.asarray(kvd), chunk


def _retention(qkr, vr, g_ret, log_gamma, batch, seq):
    cr = RET_CHUNK
    intra, inner, kvd, chunk_decay = _retention_tables(log_gamma, cr)
    step_rows = cr * RET_CHUNKS_PER_STEP
    rows = pl.BlockSpec((1, step_rows, D_MODEL), lambda b, c: (b, c, 0))
    return pl.pallas_call(
        functools.partial(_retention_kernel, chunk_decay=chunk_decay),
        grid=(batch, seq // step_rows),
        in_specs=[rows, rows,
                  pl.BlockSpec((1, RET_V_DIM), lambda b, c: (0, 0)),
                  _resident((RET_HEADS, cr, cr)),
                  _resident((RET_HEADS, cr, RET_V_DIM)),
                  _resident((RET_HEADS, cr, RET_K_DIM))],
        out_specs=rows,
        out_shape=jax.ShapeDtypeStruct((batch, seq, D_MODEL), BF16),
        scratch_shapes=[pltpu.VMEM((RET_HEADS, RET_K_DIM, RET_V_DIM), F32)],
        compiler_params=pltpu.CompilerParams(
            dimension_semantics=("arbitrary", "arbitrary"),
            vmem_limit_bytes=V7X_VMEM_LIMIT),
        name="retention",
    )(qkr, vr, g_ret, intra, inner, kvd)


def _out_mlp_kernel(x_ref, oa_ref, or_ref, ga_ref, gbs_ref, mod_ref, g2_ref,
                    wo_ref, wu_ref, wd_ref, out_ref):
    mod = mod_ref[0]
    u = (ga_ref[...].astype(F32) * oa_ref[...].astype(F32)
         + gbs_ref[...].astype(F32) * or_ref[...].astype(F32)).astype(BF16)
    y = jnp.dot(u, wo_ref[...], preferred_element_type=F32)
    x1 = x_ref[...] + mod[2:3, :] * y
    ms = jnp.mean(x1 * x1, axis=-1, keepdims=True)
    hn = x1 * lax.rsqrt(ms + EPS) * g2_ref[...]
    hff = (hn * (1.0 + mod[4:5, :]) + mod[3:4, :]).astype(BF16)
    up = jnp.dot(hff, wu_ref[...], preferred_element_type=F32)
    act = jnp.square(jnp.maximum(up, 0.0)).astype(BF16)
    ff = jnp.dot(act, wd_ref[...], preferred_element_type=F32)
    out_ref[...] = x1 + mod[5:6, :] * ff


def _out_mlp(x2, oa, orr, ga, gbs, mod3, g2, w_out, w_up, w_down, seq):
    t, d = x2.shape
    tm = TM_MLP
    tpb = seq // tm
    row = pl.BlockSpec((tm, d), lambda i: (i, 0))
    return pl.pallas_call(
        _out_mlp_kernel,
        grid=(t // tm,),
        in_specs=[row, row, row, row, row,
                  pl.BlockSpec((1, N_MOD, d), lambda i: (i // tpb, 0, 0)),
                  pl.BlockSpec((1, d), lambda i: (0, 0)),
                  _resident((d, d)), _resident((d, D_FF)), _resident((D_FF, d))],
        out_specs=row,
        out_shape=jax.ShapeDtypeStruct((t, d), F32),
        compiler_params=pltpu.CompilerParams(
            dimension_semantics=("arbitrary",),
            vmem_limit_bytes=V7X_VMEM_LIMIT),
        name="out_mlp",
    )(x2, oa, orr, ga, gbs, mod3, g2, w_out, w_up, w_down)


def _rope_tables(seq):
    pos = jnp.arange(seq, dtype=F32)[:, None]
    da_inv = ROPE_THETA ** (-jnp.arange(0, DA_QK_DIM, 2, dtype=F32) / DA_QK_DIM)
    ang = pos * da_inv[None, :]
    cos, sin = jnp.cos(ang), jnp.sin(ang)
    zero = jnp.zeros_like(sin)
    cda = jnp.tile(cos, (1, 4))
    sda_a = jnp.tile(jnp.concatenate([-sin, zero], axis=1), (1, 2))
    sda_b = jnp.tile(jnp.concatenate([zero, sin], axis=1), (1, 2))
    ret_inv = 1.0 / (ROPE_THETA ** jnp.linspace(0.0, 1.0, RET_K_DIM // 2, dtype=F32))
    ang_r = pos * ret_inv[None, :]
    cos_r, sin_r = jnp.cos(ang_r), jnp.sin(ang_r)
    cr = jnp.concatenate([cos_r, cos_r], axis=1)
    sr = jnp.concatenate([-sin_r, sin_r], axis=1)
    return cda, sda_a, sda_b, cr, sr


def kernel(x, c, w_ada, b_ada, g_norm1, w_in, g_q, g_k, lambda_q1, lambda_k1,
           lambda_q2, lambda_k2, g_da_out, g_ret_out, w_out, g_norm2, w_up, w_down):
    batch, seq, d = x.shape
    depth = w_ada.shape[0]
    assert d == D_MODEL and w_in.shape[1:] == (D_MODEL, IN_COLS), (x.shape, w_in.shape)
    for tile in (TM_PROJ, TM_MLP, TQ, TK, RET_CHUNK * RET_CHUNKS_PER_STEP):
        assert seq % tile == 0, (seq, tile)
    assert TQ == TK and TK % TM_PROJ == 0 and TQ % CHUNK == 0
    log_gamma = np.log(1.0 - 2.0 ** (-5.0 - np.arange(RET_HEADS))).astype(np.float32)
    tabs = _rope_tables(seq)

    for l in range(depth):
        lambda_init = 0.8 - 0.6 * math.exp(-0.3 * l)
        mod, lam = _modulation(c, w_ada[l], b_ada[l][None, :],
                               lambda_q1[l][None, :], lambda_k1[l][None, :],
                               lambda_q2[l][None, :], lambda_k2[l][None, :],
                               lambda_init)
        mod3 = mod.reshape(batch, N_MOD, d)
        x2 = x.reshape(batch * seq, d)

        w_in_b = w_in[l].astype(BF16)
        w_vt = _v_weight_transposed(w_in[l])
        gq = (jnp.tile(g_q[l], 2) * (DA_QK_DIM ** -0.5 * LOG2E))[None, :]
        gk = jnp.tile(g_k[l], 2)[None, :]
        qa, ka, vt, qkr, vr, ga, gbs = _in_proj(
            x2, mod3, g_norm1[l][None, :], w_in_b, w_vt, gq, gk, tabs, batch, seq)

        g_col = (g_da_out[l] * (1.0 - lambda_init))[:, None]
        oa = _diff_attn(qa.reshape(batch, seq, d), ka.reshape(batch, seq, d), vt,
                        lam, g_col, batch, seq)
        orr = _retention(qkr.reshape(batch, seq, d), vr.reshape(batch, seq, d),
                         g_ret_out[l][None, :], log_gamma, batch, seq)

        out = _out_mlp(x2, oa.reshape(batch * seq, d), orr.reshape(batch * seq, d),
                       ga, gbs, mod3, g_norm2[l][None, :],
                       w_out[l].astype(BF16), w_up[l].astype(BF16),
                       w_down[l].astype(BF16), seq)
        x = out.reshape(batch, seq, d)
    return x
```

```python
import functools
import math

import numpy as np
import jax
import jax.numpy as jnp
from jax import lax
from jax.experimental import pallas as pl
from jax.experimental.pallas import tpu as pltpu

D_MODEL = 1024
CHUNK = 64
ROPE_THETA = 10000.0
EPS = 1e-6
DA_HEADS = 8
DA_V_DIM = D_MODEL // DA_HEADS
DA_QK_DIM = DA_V_DIM // 2
RET_HEADS = 4
RET_V_DIM = D_MODEL // RET_HEADS
RET_K_DIM = RET_V_DIM // 2
D_FF = 4 * D_MODEL
N_MOD = 6
IN_COLS = 8 * D_MODEL

C_QA, C_KA, C_VA, C_QR, C_KR, C_VR, C_GR, C_GA, C_GB = (
    0, 1024, 2048, 3072, 3584, 4096, 5120, 6144, 7168)

LANES = 128
V7X_VMEM_BYTES = 64 * 1024 * 1024
V7X_VMEM_LIMIT = V7X_VMEM_BYTES - 8 * 1024 * 1024

MOD_COL_BLOCK = 1536
ROPE_HALF = DA_QK_DIM // 2

TM_PROJ = 512
TQ = 512
TK = 512
ATTN_HEADS_PER_STEP = 4
BOUND_MARGIN = 1.03
ROW_SUM_FLOOR = 2.0 ** -60
LOG2E = math.log2(math.e)
RET_CHUNK = 256
RET_CHUNKS_PER_STEP = 8
TM_MLP = 512

F32 = jnp.float32
BF16 = jnp.bfloat16
NT_DIMS = (((1,), (1,)), ((), ()))
TN_DIMS = (((0,), (0,)), ((), ()))


def _resident(shape):
    return pl.BlockSpec(shape, lambda *_: (0,) * len(shape),
                        pipeline_mode=pl.Buffered(1))


def _mod_kernel(c_ref, w_ref, b_ref, lq1_ref, lk1_ref, lq2_ref, lk2_ref,
                mod_ref, lam_ref, *, lambda_init):
    c = c_ref[...]
    sc = (c * jax.nn.sigmoid(c)).astype(BF16)
    mod_ref[...] = jnp.dot(sc, w_ref[...].astype(BF16),
                           preferred_element_type=F32) + b_ref[...]
    s1 = jnp.sum(lq1_ref[...] * lk1_ref[...], axis=-1, keepdims=True)
    s2 = jnp.sum(lq2_ref[...] * lk2_ref[...], axis=-1, keepdims=True)
    lam_ref[...] = jnp.exp(s1) - jnp.exp(s2) + lambda_init


def _modulation(c, w_ada, b_ada, lq1, lk1, lq2, lk2, lambda_init):
    b, d = c.shape
    n = w_ada.shape[1]
    bn = MOD_COL_BLOCK
    vec = pl.BlockSpec((1, DA_QK_DIM), lambda j: (0, 0))
    return pl.pallas_call(
        functools.partial(_mod_kernel, lambda_init=lambda_init),
        grid=(n // bn,),
        in_specs=[pl.BlockSpec((b, d), lambda j: (0, 0)),
                  pl.BlockSpec((d, bn), lambda j: (0, j)),
                  pl.BlockSpec((1, bn), lambda j: (0, j)),
                  vec, vec, vec, vec],
        out_specs=[pl.BlockSpec((b, bn), lambda j: (0, j)),
                   pl.BlockSpec((1, 1), lambda j: (0, 0))],
        out_shape=[jax.ShapeDtypeStruct((b, n), F32),
                   jax.ShapeDtypeStruct((1, 1), F32)],
        compiler_params=pltpu.CompilerParams(
            dimension_semantics=("arbitrary",),
            vmem_limit_bytes=V7X_VMEM_LIMIT),
        name="modulation",
    )(c, w_ada, b_ada, lq1, lk1, lq2, lk2)


def _transpose_cast_kernel(w_ref, o_ref):
    o_ref[...] = w_ref[...].T.astype(BF16)


def _v_weight_transposed(w_in_l):
    d = w_in_l.shape[0]
    return pl.pallas_call(
        _transpose_cast_kernel,
        grid=(1,),
        in_specs=[pl.BlockSpec((d, d), lambda i: (0, C_VA // d))],
        out_specs=pl.BlockSpec((d, d), lambda i: (0, 0)),
        out_shape=jax.ShapeDtypeStruct((d, d), BF16),
        compiler_params=pltpu.CompilerParams(
            dimension_semantics=("arbitrary",),
            vmem_limit_bytes=V7X_VMEM_LIMIT),
        name="v_weight_t",
    )(w_in_l)


def _in_proj_kernel(x_ref, mod_ref, g1_ref, w_ref, wvt_ref, gq_ref, gk_ref,
                    cda_ref, sda_a_ref, sda_b_ref, cr_ref, sr_ref,
                    qa_ref, ka_ref, vt_ref, qkr_ref, vr_ref, ga_ref, gbs_ref):
    tm = x_ref.shape[0]
    x = x_ref[...]
    ms = jnp.mean(x * x, axis=-1, keepdims=True)
    h = x * lax.rsqrt(ms + EPS) * g1_ref[...]
    mod = mod_ref[0]
    hm = (h * (1.0 + mod[1:2, :]) + mod[0:1, :]).astype(BF16)

    def proj(col, width):
        return jnp.dot(hm, w_ref[:, col:col + width], preferred_element_type=F32)

    cda, sda_a, sda_b = cda_ref[...], sda_a_ref[...], sda_b_ref[...]
    first_map = lax.broadcasted_iota(jnp.int32, (tm, LANES), 1) < DA_QK_DIM

    def qk_norm_rope(acc, gain, out_ref):
        for j in range(D_MODEL // LANES):
            a = acc[:, j * LANES:(j + 1) * LANES]
            a2 = a * a
            ms1 = jnp.sum(jnp.where(first_map, a2, 0.0), axis=-1, keepdims=True)
            ms2 = jnp.sum(jnp.where(first_map, 0.0, a2), axis=-1, keepdims=True)
            rs = lax.rsqrt(jnp.where(first_map, ms1, ms2) * (1.0 / DA_QK_DIM) + EPS)
            a = a * rs * gain
            rot = (a * cda + pltpu.roll(a, LANES - ROPE_HALF, 1) * sda_a
                   + pltpu.roll(a, ROPE_HALF, 1) * sda_b)
            out_ref[:, j * LANES:(j + 1) * LANES] = rot.astype(BF16)

    qk_norm_rope(proj(C_QA, D_MODEL), gq_ref[...], qa_ref)
    qk_norm_rope(proj(C_KA, D_MODEL), gk_ref[...], ka_ref)

    vt = lax.dot_general(wvt_ref[...], hm, NT_DIMS, preferred_element_type=F32)
    vt_ref[0, 0] = vt.astype(BF16)

    qkr = proj(C_QR, D_MODEL)
    cr, sr = cr_ref[...], sr_ref[...]
    for j in range(D_MODEL // LANES):
        a = qkr[:, j * LANES:(j + 1) * LANES]
        rot = a * cr + pltpu.roll(a, LANES // 2, 1) * sr
        if j >= RET_HEADS:
            rot = rot * (RET_K_DIM ** -0.5)
        qkr_ref[:, j * LANES:(j + 1) * LANES] = rot.astype(BF16)

    vr_ref[...] = proj(C_VR, D_MODEL).astype(BF16)

    gr = proj(C_GR, D_MODEL)
    ga_ref[...] = jax.nn.sigmoid(proj(C_GA, D_MODEL)).astype(BF16)
    gbs_ref[...] = (jax.nn.sigmoid(proj(C_GB, D_MODEL))
                    * (gr * jax.nn.sigmoid(gr))).astype(BF16)


def _in_proj(x2, mod3, g1, w_in, w_vt, gq, gk, tabs, batch, seq):
    t, d = x2.shape
    tm = TM_PROJ
    tpb = seq // tm
    row = pl.BlockSpec((tm, d), lambda i: (i, 0))
    tab = pl.BlockSpec((tm, LANES), lambda i: (i % tpb, 0))
    vec = pl.BlockSpec((1, LANES), lambda i: (0, 0))
    out_row = jax.ShapeDtypeStruct((t, d), BF16)
    return pl.pallas_call(
        _in_proj_kernel,
        grid=(t // tm,),
        in_specs=[row,
                  pl.BlockSpec((1, N_MOD, d), lambda i: (i // tpb, 0, 0)),
                  pl.BlockSpec((1, d), lambda i: (0, 0)),
                  _resident((d, IN_COLS)),
                  _resident((d, d)),
                  vec, vec, tab, tab, tab, tab, tab],
        out_specs=[row, row,
                   pl.BlockSpec((1, 1, d, tm),
                                lambda i: (i // tpb, (i % tpb) // (TK // tm), 0,
                                           (i % tpb) % (TK // tm))),
                   row, row, row, row],
        out_shape=[out_row, out_row,
                   jax.ShapeDtypeStruct((batch, seq // TK, d, TK), BF16),
                   out_row, out_row, out_row, out_row],
        compiler_params=pltpu.CompilerParams(
            dimension_semantics=("arbitrary",),
            vmem_limit_bytes=V7X_VMEM_LIMIT),
        name="in_proj",
    )(x2, mod3, g1, w_in, w_vt, gq, gk, *tabs)


def _diff_attn_kernel(q_ref, k_ref, vt_ref, bias_ref, lam_ref, g_ref, o_ref,
                      qcat_ref, acc_ref, l_ref, m_ref, bound_ref, kmax_ref):
    qi = pl.program_id(2)
    heads = range(ATTN_HEADS_PER_STEP)

    def k_block(j, hd):
        koff = pl.multiple_of(j * TK, TK)
        return k_ref[0, pl.ds(koff, TK), hd * LANES:(hd + 1) * LANES]

    def v_block(j, hd):
        return vt_ref[0, j, hd * DA_V_DIM:(hd + 1) * DA_V_DIM, :]

    @pl.when(qi == 0)
    def _():
        gi = lax.broadcasted_iota(jnp.int32, (LANES, LANES), 0) // DA_QK_DIM
        gj = lax.broadcasted_iota(jnp.int32, (LANES, LANES), 1) // DA_QK_DIM
        group_sum = jnp.where(gi == gj, 1.0, 0.0).astype(BF16)
        for hd in heads:
            k = k_ref[0, :, hd * LANES:(hd + 1) * LANES].astype(F32)
            norm2 = jnp.dot((k * k).astype(BF16), group_sum, preferred_element_type=F32)
            kmax_ref[hd] = jnp.max(norm2, axis=0, keepdims=True)

    def set_bounds():
        lane = lax.broadcasted_iota(jnp.int32, (TQ, LANES), 1)
        for hd in heads:
            q = q_ref[0, :, hd * LANES:(hd + 1) * LANES]
            zero = jnp.zeros_like(q)
            qcat = jnp.concatenate([jnp.where(lane < DA_QK_DIM, q, zero),
                                    jnp.where(lane >= DA_QK_DIM, q, zero)], axis=0)
            qcat_ref[hd] = qcat
            qsq = jnp.square(qcat.astype(F32)).astype(BF16)
            kmax = jnp.broadcast_to(kmax_ref[hd], (16, LANES)).astype(BF16)
            ub2 = lax.dot_general(kmax, qsq, NT_DIMS, preferred_element_type=F32)
            bound_ref[hd] = jnp.sqrt(ub2[0:1, :] * BOUND_MARGIN)

    def stream(blocks, with_diagonal):
        for hd in heads:
            pv, row_sum = None, None
            for n, j in enumerate(blocks):
                s = lax.dot_general(k_block(j, hd), qcat_ref[hd], NT_DIMS,
                                    preferred_element_type=F32)
                if with_diagonal and n == len(blocks) - 1:
                    s = s + bias_ref[...]
                e = jnp.exp2(s - bound_ref[hd])
                part_sum = jnp.sum(e, axis=0, keepdims=True)
                part = jnp.dot(v_block(j, hd), e.astype(BF16),
                               preferred_element_type=F32)
                pv = part if pv is None else pv + part
                row_sum = part_sum if row_sum is None else row_sum + part_sum
            acc_ref[hd] = pv if with_diagonal else acc_ref[hd] + pv
            l_ref[hd] = row_sum if with_diagonal else l_ref[hd] + row_sum

    @pl.when(qi % 2 == 0)
    def _():
        set_bounds()
        stream([qi], True)

    @pl.when(qi % 2 == 1)
    def _():
        set_bounds()
        stream([qi - 1, qi], True)

    def stream_body(t, carry):
        stream([2 * t, 2 * t + 1], False)
        return carry

    lax.fori_loop(0, qi // 2, stream_body, 0)

    def finalize():
        for hd in heads:
            inv_l = 1.0 / l_ref[hd]
            o = (acc_ref[hd, :, :TQ] * inv_l[:, :TQ]
                 - acc_ref[hd, :, TQ:] * (lam_ref[...] * inv_l[:, TQ:]))
            ms = jnp.mean(o * o, axis=0, keepdims=True)
            on = o * lax.rsqrt(ms + EPS) * g_ref[...]
            o_ref[0, :, hd * LANES:(hd + 1) * LANES] = on.T.astype(BF16)

    finalize()
    l_min = jnp.min(l_ref[...])

    def online_step(j, carry):
        for hd in heads:
            s = lax.dot_general(k_block(j, hd), qcat_ref[hd], NT_DIMS,
                                preferred_element_type=F32)
            s = s + jnp.where(j == qi, bias_ref[...], 0.0)
            m_old = m_ref[hd]
            m_new = jnp.maximum(m_old, jnp.max(s, axis=0, keepdims=True))
            alpha = jnp.exp2(m_old - m_new)
            e = jnp.exp2(s - m_new)
            pv = jnp.dot(v_block(j, hd), e.astype(BF16), preferred_element_type=F32)
            acc_ref[hd] = alpha * acc_ref[hd] + pv
            l_ref[hd] = alpha * l_ref[hd] + jnp.sum(e, axis=0, keepdims=True)
            m_ref[hd] = m_new
        return carry

    @pl.when(jnp.logical_not(l_min >= ROW_SUM_FLOOR))
    def _():
        m_ref[...] = jnp.full(m_ref.shape, -jnp.inf, F32)
        acc_ref[...] = jnp.zeros(acc_ref.shape, F32)
        l_ref[...] = jnp.zeros(l_ref.shape, F32)
        lax.fori_loop(0, qi + 1, online_step, 0)
        finalize()


def _diag_bias():
    kc = np.arange(TK)[:, None] // CHUNK
    qc = (np.arange(2 * TQ)[None, :] % TQ) // CHUNK
    return jnp.asarray(np.where(kc <= qc, 0.0, -np.inf).astype(np.float32))


def _diff_attn(qa, ka, vt, lam, g_col, batch, seq):
    hps = ATTN_HEADS_PER_STEP
    nq = seq // TQ
    return pl.pallas_call(
        _diff_attn_kernel,
        grid=(batch, DA_HEADS // hps, nq),
        in_specs=[pl.BlockSpec((1, TQ, hps * LANES), lambda b, h, i: (b, i, h)),
                  pl.BlockSpec((1, seq, hps * LANES), lambda b, h, i: (b, 0, h)),
                  pl.BlockSpec((1, seq // TK, hps * DA_V_DIM, TK), lambda b, h, i: (b, 0, h, 0)),
                  _resident((TK, 2 * TQ)),
                  pl.BlockSpec((1, 1), lambda b, h, i: (0, 0)),
                  pl.BlockSpec((DA_V_DIM, 1), lambda b, h, i: (0, 0))],
        out_specs=pl.BlockSpec((1, TQ, hps * LANES), lambda b, h, i: (b, i, h)),
        out_shape=jax.ShapeDtypeStruct((batch, seq, D_MODEL), BF16),
        scratch_shapes=[pltpu.VMEM((hps, 2 * TQ, LANES), BF16),
                        pltpu.VMEM((hps, DA_V_DIM, 2 * TQ), F32),
                        pltpu.VMEM((hps, 1, 2 * TQ), F32),
                        pltpu.VMEM((hps, 1, 2 * TQ), F32),
                        pltpu.VMEM((hps, 1, 2 * TQ), F32),
                        pltpu.VMEM((hps, 1, LANES), F32)],
        compiler_params=pltpu.CompilerParams(
            dimension_semantics=("arbitrary", "arbitrary", "arbitrary"),
            vmem_limit_bytes=V7X_VMEM_LIMIT),
        name="diff_attn",
    )(qa, ka, vt, _diag_bias(), lam, g_col)


def _retention_kernel(qk_ref, v_ref, g_ref, intra_ref, inner_ref, kvd_ref, o_ref,
                      state_ref, *, chunk_decay):
    @pl.when(pl.program_id(1) == 0)
    def _():
        state_ref[...] = jnp.zeros(state_ref.shape, F32)

    cr = intra_ref.shape[1]
    for ci in range(RET_CHUNKS_PER_STEP):
        rows = slice(ci * cr, (ci + 1) * cr)
        for hd in range(RET_HEADS):
            q = qk_ref[0, rows, hd * RET_K_DIM:(hd + 1) * RET_K_DIM]
            k = qk_ref[0, rows, (RET_HEADS + hd) * RET_K_DIM:(RET_HEADS + hd + 1) * RET_K_DIM]
            v = v_ref[0, rows, hd * RET_V_DIM:(hd + 1) * RET_V_DIM]
            sc = lax.dot_general(q, k, NT_DIMS, preferred_element_type=F32) * intra_ref[hd]
            y = jnp.dot(sc.astype(BF16), v, preferred_element_type=F32)
            st = state_ref[hd]
            cross = jnp.dot(q, st.astype(BF16), preferred_element_type=F32)
            y = y + cross * inner_ref[hd]
            kd = (k.astype(F32) * kvd_ref[hd]).astype(BF16)
            upd = lax.dot_general(kd, v, TN_DIMS, preferred_element_type=F32)
            state_ref[hd] = st * chunk_decay[hd] + upd
            ms = jnp.mean(y * y, axis=-1, keepdims=True)
            o_ref[0, rows, hd * RET_V_DIM:(hd + 1) * RET_V_DIM] = (
                y * lax.rsqrt(ms + EPS) * g_ref[...]).astype(BF16)


def _retention_tables(log_gamma, cr):
    n = np.arange(cr, dtype=np.float32)
    lg = log_gamma.astype(np.float32)[:, None, None]
    diff = n[:, None] - n[None, :]
    intra = np.where(diff >= 0, np.exp(np.maximum(diff, 0.0) * lg), 0.0).astype(np.float32)
    inner = np.exp((n + 1.0)[None, :, None] * lg).astype(np.float32)
    kvd = np.exp((cr - 1.0 - n)[None, :, None] * lg).astype(np.float32)
    inner = np.broadcast_to(inner, (RET_HEADS, cr, RET_V_DIM))
    kvd = np.broadcast_to(kvd, (RET_HEADS, cr, RET_K_DIM))
    chunk = tuple(float(np.exp(np.float32(cr) * g)) for g in log_gamma.astype(np.float32))
    return jnp.asarray(intra), jnp.asarray(inner), jnp.asarray(kvd), chunk


def _retention(qkr, vr, g_ret, log_gamma, batch, seq):
    cr = RET_CHUNK
    intra, inner, kvd, chunk_decay = _retention_tables(log_gamma, cr)
    step_rows = cr * RET_CHUNKS_PER_STEP
    rows = pl.BlockSpec((1, step_rows, D_MODEL), lambda b, c: (b, c, 0))
    return pl.pallas_call(
        functools.partial(_retention_kernel, chunk_decay=chunk_decay),
        grid=(batch, seq // step_rows),
        in_specs=[rows, rows,
                  pl.BlockSpec((1, RET_V_DIM), lambda b, c: (0, 0)),
                  _resident((RET_HEADS, cr, cr)),
                  _resident((RET_HEADS, cr, RET_V_DIM)),
                  _resident((RET_HEADS, cr, RET_K_DIM))],
        out_specs=rows,
        out_shape=jax.ShapeDtypeStruct((batch, seq, D_MODEL), BF16),
        scratch_shapes=[pltpu.VMEM((RET_HEADS, RET_K_DIM, RET_V_DIM), F32)],
        compiler_params=pltpu.CompilerParams(
            dimension_semantics=("arbitrary", "arbitrary"),
            vmem_limit_bytes=V7X_VMEM_LIMIT),
        name="retention",
    )(qkr, vr, g_ret, intra, inner, kvd)


def _out_mlp_kernel(x_ref, oa_ref, or_ref, ga_ref, gbs_ref, mod_ref, g2_ref,
                    wo_ref, wu_ref, wd_ref, out_ref):
    mod = mod_ref[0]
    u = (ga_ref[...].astype(F32) * oa_ref[...].astype(F32)
         + gbs_ref[...].astype(F32) * or_ref[...].astype(F32)).astype(BF16)
    y = jnp.dot(u, wo_ref[...], preferred_element_type=F32)
    x1 = x_ref[...] + mod[2:3, :] * y
    ms = jnp.mean(x1 * x1, axis=-1, keepdims=True)
    hn = x1 * lax.rsqrt(ms + EPS) * g2_ref[...]
    hff = (hn * (1.0 + mod[4:5, :]) + mod[3:4, :]).astype(BF16)
    up = jnp.dot(hff, wu_ref[...], preferred_element_type=F32)
    act = jnp.square(jnp.maximum(up, 0.0)).astype(BF16)
    ff = jnp.dot(act, wd_ref[...], preferred_element_type=F32)
    out_ref[...] = x1 + mod[5:6, :] * ff


def _out_mlp(x2, oa, orr, ga, gbs, mod3, g2, w_out, w_up, w_down, seq):
    t, d = x2.shape
    tm = TM_MLP
    tpb = seq // tm
    row = pl.BlockSpec((tm, d), lambda i: (i, 0))
    return pl.pallas_call(
        _out_mlp_kernel,
        grid=(t // tm,),
        in_specs=[row, row, row, row, row,
                  pl.BlockSpec((1, N_MOD, d), lambda i: (i // tpb, 0, 0)),
                  pl.BlockSpec((1, d), lambda i: (0, 0)),
                  _resident((d, d)), _resident((d, D_FF)), _resident((D_FF, d))],
        out_specs=row,
        out_shape=jax.ShapeDtypeStruct((t, d), F32),
        compiler_params=pltpu.CompilerParams(
            dimension_semantics=("arbitrary",),
            vmem_limit_bytes=V7X_VMEM_LIMIT),
        name="out_mlp",
    )(x2, oa, orr, ga, gbs, mod3, g2, w_out, w_up, w_down)


def _rope_tables(seq):
    pos = jnp.arange(seq, dtype=F32)[:, None]
    da_inv = ROPE_THETA ** (-jnp.arange(0, DA_QK_DIM, 2, dtype=F32) / DA_QK_DIM)
    ang = pos * da_inv[None, :]
    cos, sin = jnp.cos(ang), jnp.sin(ang)
    zero = jnp.zeros_like(sin)
    cda = jnp.tile(cos, (1, 4))
    sda_a = jnp.tile(jnp.concatenate([-sin, zero], axis=1), (1, 2))
    sda_b = jnp.tile(jnp.concatenate([zero, sin], axis=1), (1, 2))
    ret_inv = 1.0 / (ROPE_THETA ** jnp.linspace(0.0, 1.0, RET_K_DIM // 2, dtype=F32))
    ang_r = pos * ret_inv[None, :]
    cos_r, sin_r = jnp.cos(ang_r), jnp.sin(ang_r)
    cr = jnp.concatenate([cos_r, cos_r], axis=1)
    sr = jnp.concatenate([-sin_r, sin_r], axis=1)
    return cda, sda_a, sda_b, cr, sr


def kernel(x, c, w_ada, b_ada, g_norm1, w_in, g_q, g_k, lambda_q1, lambda_k1,
           lambda_q2, lambda_k2, g_da_out, g_ret_out, w_out, g_norm2, w_up, w_down):
    batch, seq, d = x.shape
    depth = w_ada.shape[0]
    assert d == D_MODEL and w_in.shape[1:] == (D_MODEL, IN_COLS), (x.shape, w_in.shape)
    for tile in (TM_PROJ, TM_MLP, TQ, TK, RET_CHUNK * RET_CHUNKS_PER_STEP):
        assert seq % tile == 0, (seq, tile)
    assert TQ == TK and TK % TM_PROJ == 0 and TQ % CHUNK == 0
    log_gamma = np.log(1.0 - 2.0 ** (-5.0 - np.arange(RET_HEADS))).astype(np.float32)
    tabs = _rope_tables(seq)

    for l in range(depth):
        lambda_init = 0.8 - 0.6 * math.exp(-0.3 * l)
        mod, lam = _modulation(c, w_ada[l], b_ada[l][None, :],
                               lambda_q1[l][None, :], lambda_k1[l][None, :],
                               lambda_q2[l][None, :], lambda_k2[l][None, :],
                               lambda_init)
        mod3 = mod.reshape(batch, N_MOD, d)
        x2 = x.reshape(batch * seq, d)

        w_in_b = w_in[l].astype(BF16)
        w_vt = _v_weight_transposed(w_in[l])
        gq = (jnp.tile(g_q[l], 2) * (DA_QK_DIM ** -0.5 * LOG2E))[None, :]
        gk = jnp.tile(g_k[l], 2)[None, :]
        qa, ka, vt, qkr, vr, ga, gbs = _in_proj(
            x2, mod3, g_norm1[l][None, :], w_in_b, w_vt, gq, gk, tabs, batch, seq)

        g_col = (g_da_out[l] * (1.0 - lambda_init))[:, None]
        oa = _diff_attn(qa.reshape(batch, seq, d), ka.reshape(batch, seq, d), vt,
                        lam, g_col, batch, seq)
        orr = _retention(qkr.reshape(batch, seq, d), vr.reshape(batch, seq, d),
                         g_ret_out[l][None, :], log_gamma, batch, seq)

        out = _out_mlp(x2, oa.reshape(batch * seq, d), orr.reshape(batch * seq, d),
                       ga, gbs, mod3, g_norm2[l][None, :],
                       w_out[l].astype(BF16), w_up[l].astype(BF16),
                       w_down[l].astype(BF16), seq)
        x = out.reshape(batch, seq, d)
    return x
```

```python
import functools
import math

import numpy as np
import jax
import jax.numpy as jnp
from jax import lax
from jax.experimental import pallas as pl
from jax.experimental.pallas import tpu as pltpu

D_MODEL = 1024
CHUNK = 64
ROPE_THETA = 10000.0
EPS = 1e-6
DA_HEADS = 8
DA_V_DIM = D_MODEL // DA_HEADS
DA_QK_DIM = DA_V_DIM // 2
RET_HEADS = 4
RET_V_DIM = D_MODEL // RET_HEADS
RET_K_DIM = RET_V_DIM // 2
D_FF = 4 * D_MODEL
N_MOD = 6
IN_COLS = 8 * D_MODEL

C_QA, C_KA, C_VA, C_QR, C_KR, C_VR, C_GR, C_GA, C_GB = (
    0, 1024, 2048, 3072, 3584, 4096, 5120, 6144, 7168)

LANES = 128
V7X_VMEM_BYTES = 64 * 1024 * 1024
V7X_VMEM_LIMIT = V7X_VMEM_BYTES - 8 * 1024 * 1024

MOD_COL_BLOCK = 1536
ROPE_HALF = DA_QK_DIM // 2

TM_PROJ = 512
TQ = 512
TK = 512
ATTN_HEADS_PER_STEP = 4
BOUND_MARGIN = 1.03
ROW_SUM_FLOOR = 2.0 ** -60
LOG2E = math.log2(math.e)
RET_CHUNK = 256
RET_CHUNKS_PER_STEP = 8
TM_MLP = 512

F32 = jnp.float32
BF16 = jnp.bfloat16
NT_DIMS = (((1,), (1,)), ((), ()))
TN_DIMS = (((0,), (0,)), ((), ()))


def _resident(shape):
    return pl.BlockSpec(shape, lambda *_: (0,) * len(shape),
                        pipeline_mode=pl.Buffered(1))


def _mod_kernel(c_ref, w_ref, b_ref, lq1_ref, lk1_ref, lq2_ref, lk2_ref,
                mod_ref, lam_ref, *, lambda_init):
    c = c_ref[...]
    sc = (c * jax.nn.sigmoid(c)).astype(BF16)
    mod_ref[...] = jnp.dot(sc, w_ref[...].astype(BF16),
                           preferred_element_type=F32) + b_ref[...]
    s1 = jnp.sum(lq1_ref[...] * lk1_ref[...], axis=-1, keepdims=True)
    s2 = jnp.sum(lq2_ref[...] * lk2_ref[...], axis=-1, keepdims=True)
    lam_ref[...] = jnp.exp(s1) - jnp.exp(s2) + lambda_init


def _modulation(c, w_ada, b_ada, lq1, lk1, lq2, lk2, lambda_init):
    b, d = c.shape
    n = w_ada.shape[1]
    bn = MOD_COL_BLOCK
    vec = pl.BlockSpec((1, DA_QK_DIM), lambda j: (0, 0))
    return pl.pallas_call(
        functools.partial(_mod_kernel, lambda_init=lambda_init),
        grid=(n // bn,),
        in_specs=[pl.BlockSpec((b, d), lambda j: (0, 0)),
                  pl.BlockSpec((d, bn), lambda j: (0, j)),
                  pl.BlockSpec((1, bn), lambda j: (0, j)),
                  vec, vec, vec, vec],
        out_specs=[pl.BlockSpec((b, bn), lambda j: (0, j)),
                   pl.BlockSpec((1, 1), lambda j: (0, 0))],
        out_shape=[jax.ShapeDtypeStruct((b, n), F32),
                   jax.ShapeDtypeStruct((1, 1), F32)],
        compiler_params=pltpu.CompilerParams(
            dimension_semantics=("arbitrary",),
            vmem_limit_bytes=V7X_VMEM_LIMIT),
        name="modulation",
    )(c, w_ada, b_ada, lq1, lk1, lq2, lk2)


def _transpose_cast_kernel(w_ref, o_ref):
    o_ref[...] = w_ref[...].T.astype(BF16)


def _v_weight_transposed(w_in_l):
    d = w_in_l.shape[0]
    return pl.pallas_call(
        _transpose_cast_kernel,
        grid=(1,),
        in_specs=[pl.BlockSpec((d, d), lambda i: (0, C_VA // d))],
        out_specs=pl.BlockSpec((d, d), lambda i: (0, 0)),
        out_shape=jax.ShapeDtypeStruct((d, d), BF16),
        compiler_params=pltpu.CompilerParams(
            dimension_semantics=("arbitrary",),
            vmem_limit_bytes=V7X_VMEM_LIMIT),
        name="v_weight_t",
    )(w_in_l)


def _in_proj_kernel(x_ref, mod_ref, g1_ref, w_ref, wvt_ref, gq_ref, gk_ref,
                    cda_ref, sda_a_ref, sda_b_ref, cr_ref, sr_ref,
                    qa_ref, ka_ref, vt_ref, qkr_ref, vr_ref, ga_ref, gbs_ref):
    tm = x_ref.shape[0]
    x = x_ref[...]
    ms = jnp.mean(x * x, axis=-1, keepdims=True)
    h = x * lax.rsqrt(ms + EPS) * g1_ref[...]
    mod = mod_ref[0]
    hm = (h * (1.0 + mod[1:2, :]) + mod[0:1, :]).astype(BF16)

    def proj(col, width):
        return jnp.dot(hm, w_ref[:, col:col + width], preferred_element_type=F32)

    cda, sda_a, sda_b = cda_ref[...], sda_a_ref[...], sda_b_ref[...]
    first_map = lax.broadcasted_iota(jnp.int32, (tm, LANES), 1) < DA_QK_DIM

    def qk_norm_rope(acc, gain, out_ref):
        for j in range(D_MODEL // LANES):
            a = acc[:, j * LANES:(j + 1) * LANES]
            a2 = a * a
            ms1 = jnp.sum(jnp.where(first_map, a2, 0.0), axis=-1, keepdims=True)
            ms2 = jnp.sum(jnp.where(first_map, 0.0, a2), axis=-1, keepdims=True)
            rs = lax.rsqrt(jnp.where(first_map, ms1, ms2) * (1.0 / DA_QK_DIM) + EPS)
            a = a * rs * gain
            rot = (a * cda + pltpu.roll(a, LANES - ROPE_HALF, 1) * sda_a
                   + pltpu.roll(a, ROPE_HALF, 1) * sda_b)
            out_ref[:, j * LANES:(j + 1) * LANES] = rot.astype(BF16)

    qk_norm_rope(proj(C_QA, D_MODEL), gq_ref[...], qa_ref)
    qk_norm_rope(proj(C_KA, D_MODEL), gk_ref[...], ka_ref)

    vt = lax.dot_general(wvt_ref[...], hm, NT_DIMS, preferred_element_type=F32)
    vt_ref[0, 0] = vt.astype(BF16)

    qkr = proj(C_QR, D_MODEL)
    cr, sr = cr_ref[...], sr_ref[...]
    for j in range(D_MODEL // LANES):
        a = qkr[:, j * LANES:(j + 1) * LANES]
        rot = a * cr + pltpu.roll(a, LANES // 2, 1) * sr
        if j >= RET_HEADS:
            rot = rot * (RET_K_DIM ** -0.5)
        qkr_ref[:, j * LANES:(j + 1) * LANES] = rot.astype(BF16)

    vr_ref[...] = proj(C_VR, D_MODEL).astype(BF16)

    gr = proj(C_GR, D_MODEL)
    ga_ref[...] = jax.nn.sigmoid(proj(C_GA, D_MODEL)).astype(BF16)
    gbs_ref[...] = (jax.nn.sigmoid(proj(C_GB, D_MODEL))
                    * (gr * jax.nn.sigmoid(gr))).astype(BF16)


def _in_proj(x2, mod3, g1, w_in, w_vt, gq, gk, tabs, batch, seq):
    t, d = x2.shape
    tm = TM_PROJ
    tpb = seq // tm
    row = pl.BlockSpec((tm, d), lambda i: (i, 0))
    tab = pl.BlockSpec((tm, LANES), lambda i: (i % tpb, 0))
    vec = pl.BlockSpec((1, LANES), lambda i: (0, 0))
    out_row = jax.ShapeDtypeStruct((t, d), BF16)
    return pl.pallas_call(
        _in_proj_kernel,
        grid=(t // tm,),
        in_specs=[row,
                  pl.BlockSpec((1, N_MOD, d), lambda i: (i // tpb, 0, 0)),
                  pl.BlockSpec((1, d), lambda i: (0, 0)),
                  _resident((d, IN_COLS)),
                  _resident((d, d)),
                  vec, vec, tab, tab, tab, tab, tab],
        out_specs=[row, row,
                   pl.BlockSpec((1, 1, d, tm),
                                lambda i: (i // tpb, (i % tpb) // (TK // tm), 0,
                                           (i % tpb) % (TK // tm))),
                   row, row, row, row],
        out_shape=[out_row, out_row,
                   jax.ShapeDtypeStruct((batch, seq // TK, d, TK), BF16),
                   out_row, out_row, out_row, out_row],
        compiler_params=pltpu.CompilerParams(
            dimension_semantics=("arbitrary",),
            vmem_limit_bytes=V7X_VMEM_LIMIT),
        name="in_proj",
    )(x2, mod3, g1, w_in, w_vt, gq, gk, *tabs)


def _diff_attn_kernel(q_ref, k_ref, vt_ref, bias_ref, lam_ref, g_ref, o_ref,
                      qcat_ref, acc_ref, l_ref, m_ref, bound_ref, kmax_ref):
    qi = pl.program_id(2)
    heads = range(ATTN_HEADS_PER_STEP)

    def k_block(j, hd):
        koff = pl.multiple_of(j * TK, TK)
        return k_ref[0, pl.ds(koff, TK), hd * LANES:(hd + 1) * LANES]

    def v_block(j, hd):
        return vt_ref[0, j, hd * DA_V_DIM:(hd + 1) * DA_V_DIM, :]

    @pl.when(qi == 0)
    def _():
        gi = lax.broadcasted_iota(jnp.int32, (LANES, LANES), 0) // DA_QK_DIM
        gj = lax.broadcasted_iota(jnp.int32, (LANES, LANES), 1) // DA_QK_DIM
        group_sum = jnp.where(gi == gj, 1.0, 0.0).astype(BF16)
        for hd in heads:
            k = k_ref[0, :, hd * LANES:(hd + 1) * LANES].astype(F32)
            norm2 = jnp.dot((k * k).astype(BF16), group_sum, preferred_element_type=F32)
            kmax_ref[hd] = jnp.max(norm2, axis=0, keepdims=True)

    def set_bounds():
        lane = lax.broadcasted_iota(jnp.int32, (TQ, LANES), 1)
        for hd in heads:
            q = q_ref[0, :, hd * LANES:(hd + 1) * LANES]
            zero = jnp.zeros_like(q)
            qcat = jnp.concatenate([jnp.where(lane < DA_QK_DIM, q, zero),
                                    jnp.where(lane >= DA_QK_DIM, q, zero)], axis=0)
            qcat_ref[hd] = qcat
            qsq = jnp.square(qcat.astype(F32)).astype(BF16)
            kmax = jnp.broadcast_to(kmax_ref[hd], (16, LANES)).astype(BF16)
            ub2 = lax.dot_general(kmax, qsq, NT_DIMS, preferred_element_type=F32)
            bound_ref[hd] = jnp.sqrt(ub2[0:1, :] * BOUND_MARGIN)

    def stream(blocks, with_diagonal):
        for hd in heads:
            pv, row_sum = None, None
            for n, j in enumerate(blocks):
                s = lax.dot_general(k_block(j, hd), qcat_ref[hd], NT_DIMS,
                                    preferred_element_type=F32)
                if with_diagonal and n == len(blocks) - 1:
                    s = s + bias_ref[...]
                e = jnp.exp2(s - bound_ref[hd])
                part_sum = jnp.sum(e, axis=0, keepdims=True)
                part = jnp.dot(v_block(j, hd), e.astype(BF16),
                               preferred_element_type=F32)
                pv = part if pv is None else pv + part
                row_sum = part_sum if row_sum is None else row_sum + part_sum
            acc_ref[hd] = pv if with_diagonal else acc_ref[hd] + pv
            l_ref[hd] = row_sum if with_diagonal else l_ref[hd] + row_sum

    @pl.when(qi % 2 == 0)
    def _():
        set_bounds()
        stream([qi], True)

    @pl.when(qi % 2 == 1)
    def _():
        set_bounds()
        stream([qi - 1, qi], True)

    def stream_body(t, carry):
        stream([2 * t, 2 * t + 1], False)
        return carry

    lax.fori_loop(0, qi // 2, stream_body, 0)

    def finalize():
        for hd in heads:
            inv_l = 1.0 / l_ref[hd]
            o = (acc_ref[hd, :, :TQ] * inv_l[:, :TQ]
                 - acc_ref[hd, :, TQ:] * (lam_ref[...] * inv_l[:, TQ:]))
            ms = jnp.mean(o * o, axis=0, keepdims=True)
            on = o * lax.rsqrt(ms + EPS) * g_ref[...]
            o_ref[0, :, hd * LANES:(hd + 1) * LANES] = on.T.astype(BF16)

    finalize()
    l_min = jnp.min(l_ref[...])

    def online_step(j, carry):
        for hd in heads:
            s = lax.dot_general(k_block(j, hd), qcat_ref[hd], NT_DIMS,
                                preferred_element_type=F32)
            s = s + jnp.where(j == qi, bias_ref[...], 0.0)
            m_old = m_ref[hd]
            m_new = jnp.maximum(m_old, jnp.max(s, axis=0, keepdims=True))
            alpha = jnp.exp2(m_old - m_new)
            e = jnp.exp2(s - m_new)
            pv = jnp.dot(v_block(j, hd), e.astype(BF16), preferred_element_type=F32)
            acc_ref[hd] = alpha * acc_ref[hd] + pv
            l_ref[hd] = alpha * l_ref[hd] + jnp.sum(e, axis=0, keepdims=True)
            m_ref[hd] = m_new
        return carry

    @pl.when(jnp.logical_not(l_min >= ROW_SUM_FLOOR))
    def _():
        m_ref[...] = jnp.full(m_ref.shape, -jnp.inf, F32)
        acc_ref[...] = jnp.zeros(acc_ref.shape, F32)
        l_ref[...] = jnp.zeros(l_ref.shape, F32)
        lax.fori_loop(0, qi + 1, online_step, 0)
        finalize()


def _diag_bias():
    kc = np.arange(TK)[:, None] // CHUNK
    qc = (np.arange(2 * TQ)[None, :] % TQ) // CHUNK
    return jnp.asarray(np.where(kc <= qc, 0.0, -np.inf).astype(np.float32))


def _diff_attn(qa, ka, vt, lam, g_col, batch, seq):
    hps = ATTN_HEADS_PER_STEP
    nq = seq // TQ
    return pl.pallas_call(
        _diff_attn_kernel,
        grid=(batch, DA_HEADS // hps, nq),
        in_specs=[pl.BlockSpec((1, TQ, hps * LANES), lambda b, h, i: (b, i, h)),
                  pl.BlockSpec((1, seq, hps * LANES), lambda b, h, i: (b, 0, h)),
                  pl.BlockSpec((1, seq // TK, hps * DA_V_DIM, TK), lambda b, h, i: (b, 0, h, 0)),
                  _resident((TK, 2 * TQ)),
                  pl.BlockSpec((1, 1), lambda b, h, i: (0, 0)),
                  pl.BlockSpec((DA_V_DIM, 1), lambda b, h, i: (0, 0))],
        out_specs=pl.BlockSpec((1, TQ, hps * LANES), lambda b, h, i: (b, i, h)),
        out_shape=jax.ShapeDtypeStruct((batch, seq, D_MODEL), BF16),
        scratch_shapes=[pltpu.VMEM((hps, 2 * TQ, LANES), BF16),
                        pltpu.VMEM((hps, DA_V_DIM, 2 * TQ), F32),
                        pltpu.VMEM((hps, 1, 2 * TQ), F32),
                        pltpu.VMEM((hps, 1, 2 * TQ), F32),
                        pltpu.VMEM((hps, 1, 2 * TQ), F32),
                        pltpu.VMEM((hps, 1, LANES), F32)],
        compiler_params=pltpu.CompilerParams(
            dimension_semantics=("arbitrary", "arbitrary", "arbitrary"),
            vmem_limit_bytes=V7X_VMEM_LIMIT),
        name="diff_attn",
    )(qa, ka, vt, _diag_bias(), lam, g_col)


def _retention_kernel(qk_ref, v_ref, g_ref, intra_ref, inner_ref, kvd_ref, o_ref,
                      state_ref, *, chunk_decay):
    @pl.when(pl.program_id(1) == 0)
    def _():
        state_ref[...] = jnp.zeros(state_ref.shape, F32)

    cr = intra_ref.shape[1]
    for ci in range(RET_CHUNKS_PER_STEP):
        rows = slice(ci * cr, (ci + 1) * cr)
        for hd in range(RET_HEADS):
            q = qk_ref[0, rows, hd * RET_K_DIM:(hd + 1) * RET_K_DIM]
            k = qk_ref[0, rows, (RET_HEADS + hd) * RET_K_DIM:(RET_HEADS + hd + 1) * RET_K_DIM]
            v = v_ref[0, rows, hd * RET_V_DIM:(hd + 1) * RET_V_DIM]
            sc = lax.dot_general(q, k, NT_DIMS, preferred_element_type=F32) * intra_ref[hd]
            y = jnp.dot(sc.astype(BF16), v, preferred_element_type=F32)
            st = state_ref[hd]
            cross = jnp.dot(q, st.astype(BF16), preferred_element_type=F32)
            y = y + cross * inner_ref[hd]
            kd = (k.astype(F32) * kvd_ref[hd]).astype(BF16)
            upd = lax.dot_general(kd, v, TN_DIMS, preferred_element_type=F32)
            state_ref[hd] = st * chunk_decay[hd] + upd
            ms = jnp.mean(y * y, axis=-1, keepdims=True)
            o_ref[0, rows, hd * RET_V_DIM:(hd + 1) * RET_V_DIM] = (
                y * lax.rsqrt(ms + EPS) * g_ref[...]).astype(BF16)


def _retention_tables(log_gamma, cr):
    n = np.arange(cr, dtype=np.float32)
    lg = log_gamma.astype(np.float32)[:, None, None]
    diff = n[:, None] - n[None, :]
    intra = np.where(diff >= 0, np.exp(np.maximum(diff, 0.0) * lg), 0.0).astype(np.float32)
    inner = np.exp((n + 1.0)[None, :, None] * lg).astype(np.float32)
    kvd = np.exp((cr - 1.0 - n)[None, :, None] * lg).astype(np.float32)
    inner = np.broadcast_to(inner, (RET_HEADS, cr, RET_V_DIM))
    kvd = np.broadcast_to(kvd, (RET_HEADS, cr, RET_K_DIM))
    chunk = tuple(float(np.exp(np.float32(cr) * g)) for g in log_gamma.astype(np.float32))
    return jnp.asarray(intra), jnp.asarray(inner), jnp.asarray(kvd), chunk


def _retention(qkr, vr, g_ret, log_gamma, batch, seq):
    cr = RET_CHUNK
    intra, inner, kvd, chunk_decay = _retention_tables(log_gamma, cr)
    step_rows = cr * RET_CHUNKS_PER_STEP
    rows = pl.BlockSpec((1, step_rows, D_MODEL), lambda b, c: (b, c, 0))
    return pl.pallas_call(
        functools.partial(_retention_kernel, chunk_decay=chunk_decay),
        grid=(batch, seq // step_rows),
        in_specs=[rows, rows,
                  pl.BlockSpec((1, RET_V_DIM), lambda b, c: (0, 0)),
                  _resident((RET_HEADS, cr, cr)),
                  _resident((RET_HEADS, cr, RET_V_DIM)),
                  _resident((RET_HEADS, cr, RET_K_DIM))],
        out_specs=rows,
        out_shape=jax.ShapeDtypeStruct((batch, seq, D_MODEL), BF16),
        scratch_shapes=[pltpu.VMEM((RET_HEADS, RET_K_DIM, RET_V_DIM), F32)],
        compiler_params=pltpu.CompilerParams(
            dimension_semantics=("arbitrary", "arbitrary"),
            vmem_limit_bytes=V7X_VMEM_LIMIT),
        name="retention",
    )(qkr, vr, g_ret, intra, inner, kvd)


def _out_mlp_kernel(x_ref, oa_ref, or_ref, ga_ref, gbs_ref, mod_ref, g2_ref,
                    wo_ref, wu_ref, wd_ref, out_ref):
    mod = mod_ref[0]
    u = (ga_ref[...].astype(F32) * oa_ref[...].astype(F32)
         + gbs_ref[...].astype(F32) * or_ref[...].astype(F32)).astype(BF16)
    y = jnp.dot(u, wo_ref[...], preferred_element_type=F32)
    x1 = x_ref[...] + mod[2:3, :] * y
    ms = jnp.mean(x1 * x1, axis=-1, keepdims=True)
    hn = x1 * lax.rsqrt(ms + EPS) * g2_ref[...]
    hff = (hn * (1.0 + mod[4:5, :]) + mod[3:4, :]).astype(BF16)
    up = jnp.dot(hff, wu_ref[...], preferred_element_type=F32)
    act = jnp.square(jnp.maximum(up, 0.0)).astype(BF16)
    ff = jnp.dot(act, wd_ref[...], preferred_element_type=F32)
    out_ref[...] = x1 + mod[5:6, :] * ff


def _out_mlp(x2, oa, orr, ga, gbs, mod3, g2, w_out, w_up, w_down, seq):
    t, d = x2.shape
    tm = TM_MLP
    tpb = seq // tm
    row = pl.BlockSpec((tm, d), lambda i: (i, 0))
    return pl.pallas_call(
        _out_mlp_kernel,
        grid=(t // tm,),
        in_specs=[row, row, row, row, row,
                  pl.BlockSpec((1, N_MOD, d), lambda i: (i // tpb, 0, 0)),
                  pl.BlockSpec((1, d), lambda i: (0, 0)),
                  _resident((d, d)), _resident((d, D_FF)), _resident((D_FF, d))],
        out_specs=row,
        out_shape=jax.ShapeDtypeStruct((t, d), F32),
        compiler_params=pltpu.CompilerParams(
            dimension_semantics=("arbitrary",),
            vmem_limit_bytes=V7X_VMEM_LIMIT),
        name="out_mlp",
    )(x2, oa, orr, ga, gbs, mod3, g2, w_out, w_up, w_down)


def _rope_tables(seq):
    f32 = np.float32
    pos = np.arange(seq, dtype=f32)[:, None]
    da_inv = (f32(ROPE_THETA) ** (-np.arange(0, DA_QK_DIM, 2, dtype=f32) / f32(DA_QK_DIM))).astype(f32)
    ang = (pos * da_inv[None, :]).astype(f32)
    cos, sin = np.cos(ang).astype(f32), np.sin(ang).astype(f32)
    zero = np.zeros_like(sin)
    cda = np.tile(cos, (1, 4))
    sda_a = np.tile(np.concatenate([-sin, zero], axis=1), (1, 2))
    sda_b = np.tile(np.concatenate([zero, sin], axis=1), (1, 2))
    ret_inv = (f32(1.0) / (f32(ROPE_THETA) ** np.linspace(0.0, 1.0, RET_K_DIM // 2, dtype=f32))).astype(f32)
    ang_r = (pos * ret_inv[None, :]).astype(f32)
    cos_r, sin_r = np.cos(ang_r).astype(f32), np.sin(ang_r).astype(f32)
    cr = np.concatenate([cos_r, cos_r], axis=1)
    sr = np.concatenate([-sin_r, sin_r], axis=1)
    return tuple(jnp.asarray(t) for t in (cda, sda_a, sda_b, cr, sr))


def kernel(x, c, w_ada, b_ada, g_norm1, w_in, g_q, g_k, lambda_q1, lambda_k1,
           lambda_q2, lambda_k2, g_da_out, g_ret_out, w_out, g_norm2, w_up, w_down):
    batch, seq, d = x.shape
    depth = w_ada.shape[0]
    assert d == D_MODEL and w_in.shape[1:] == (D_MODEL, IN_COLS), (x.shape, w_in.shape)
    for tile in (TM_PROJ, TM_MLP, TQ, TK, RET_CHUNK * RET_CHUNKS_PER_STEP):
        assert seq % tile == 0, (seq, tile)
    assert TQ == TK and TK % TM_PROJ == 0 and TQ % CHUNK == 0
    log_gamma = np.log(1.0 - 2.0 ** (-5.0 - np.arange(RET_HEADS))).astype(np.float32)
    tabs = _rope_tables(seq)

    for l in range(depth):
        lambda_init = 0.8 - 0.6 * math.exp(-0.3 * l)
        mod, lam = _modulation(c, w_ada[l], b_ada[l][None, :],
                               lambda_q1[l][None, :], lambda_k1[l][None, :],
                               lambda_q2[l][None, :], lambda_k2[l][None, :],
                               lambda_init)
        mod3 = mod.reshape(batch, N_MOD, d)
        x2 = x.reshape(batch * seq, d)

        w_in_b = w_in[l].astype(BF16)
        w_vt = _v_weight_transposed(w_in[l])
        gq = (jnp.tile(g_q[l], 2) * (DA_QK_DIM ** -0.5 * LOG2E))[None, :]
        gk = jnp.tile(g_k[l], 2)[None, :]
        qa, ka, vt, qkr, vr, ga, gbs = _in_proj(
            x2, mod3, g_norm1[l][None, :], w_in_b, w_vt, gq, gk, tabs, batch, seq)

        g_col = (g_da_out[l] * (1.0 - lambda_init))[:, None]
        oa = _diff_attn(qa.reshape(batch, seq, d), ka.reshape(batch, seq, d), vt,
                        lam, g_col, batch, seq)
        orr = _retention(qkr.reshape(batch, seq, d), vr.reshape(batch, seq, d),
                         g_ret_out[l][None, :], log_gamma, batch, seq)

        out = _out_mlp(x2, oa.reshape(batch * seq, d), orr.reshape(batch * seq, d),
                       ga, gbs, mod3, g_norm2[l][None, :],
                       w_out[l].astype(BF16), w_up[l].astype(BF16),
                       w_down[l].astype(BF16), seq)
        x = out.reshape(batch, seq, d)
    return x
```

```python
import functools
import math

import numpy as np
import jax
import jax.numpy as jnp
from jax import lax
from jax.experimental import pallas as pl
from jax.experimental.pallas import tpu as pltpu

D_MODEL = 1024
CHUNK = 64
ROPE_THETA = 10000.0
EPS = 1e-6
DA_HEADS = 8
DA_V_DIM = D_MODEL // DA_HEADS
DA_QK_DIM = DA_V_DIM // 2
RET_HEADS = 4
RET_V_DIM = D_MODEL // RET_HEADS
RET_K_DIM = RET_V_DIM // 2
D_FF = 4 * D_MODEL
N_MOD = 6
IN_COLS = 8 * D_MODEL

C_QA, C_KA, C_VA, C_QR, C_KR, C_VR, C_GR, C_GA, C_GB = (
    0, 1024, 2048, 3072, 3584, 4096, 5120, 6144, 7168)

LANES = 128
V7X_VMEM_BYTES = 64 * 1024 * 1024
V7X_VMEM_LIMIT = V7X_VMEM_BYTES - 8 * 1024 * 1024

MOD_COL_BLOCK = 1536
ROPE_HALF = DA_QK_DIM // 2

TM_PROJ = 512
TQ = 512
TK = 512
ATTN_HEADS_PER_STEP = 4
BOUND_MARGIN = 1.03
ROW_SUM_FLOOR = 2.0 ** -60
LOG2E = math.log2(math.e)
RET_CHUNK = 256
RET_CHUNKS_PER_STEP = 8
TM_MLP = 512

F32 = jnp.float32
BF16 = jnp.bfloat16
NT_DIMS = (((1,), (1,)), ((), ()))
TN_DIMS = (((0,), (0,)), ((), ()))


def _resident(shape):
    return pl.BlockSpec(shape, lambda *_: (0,) * len(shape),
                        pipeline_mode=pl.Buffered(1))


def _mod_kernel(c_ref, w_ref, b_ref, lq1_ref, lk1_ref, lq2_ref, lk2_ref,
                mod_ref, lam_ref, *, lambda_init):
    c = c_ref[...]
    sc = (c * jax.nn.sigmoid(c)).astype(BF16)
    mod_ref[...] = jnp.dot(sc, w_ref[...].astype(BF16),
                           preferred_element_type=F32) + b_ref[...]
    s1 = jnp.sum(lq1_ref[...] * lk1_ref[...], axis=-1, keepdims=True)
    s2 = jnp.sum(lq2_ref[...] * lk2_ref[...], axis=-1, keepdims=True)
    lam_ref[...] = jnp.exp(s1) - jnp.exp(s2) + lambda_init


def _modulation(c, w_ada, b_ada, lq1, lk1, lq2, lk2, lambda_init):
    b, d = c.shape
    n = w_ada.shape[1]
    bn = MOD_COL_BLOCK
    vec = pl.BlockSpec((1, DA_QK_DIM), lambda j: (0, 0))
    return pl.pallas_call(
        functools.partial(_mod_kernel, lambda_init=lambda_init),
        grid=(n // bn,),
        in_specs=[pl.BlockSpec((b, d), lambda j: (0, 0)),
                  pl.BlockSpec((d, bn), lambda j: (0, j)),
                  pl.BlockSpec((1, bn), lambda j: (0, j)),
                  vec, vec, vec, vec],
        out_specs=[pl.BlockSpec((b, bn), lambda j: (0, j)),
                   pl.BlockSpec((1, 1), lambda j: (0, 0))],
        out_shape=[jax.ShapeDtypeStruct((b, n), F32),
                   jax.ShapeDtypeStruct((1, 1), F32)],
        compiler_params=pltpu.CompilerParams(
            dimension_semantics=("arbitrary",),
            vmem_limit_bytes=V7X_VMEM_LIMIT),
        name="modulation",
    )(c, w_ada, b_ada, lq1, lk1, lq2, lk2)


def _transpose_cast_kernel(w_ref, o_ref):
    o_ref[...] = w_ref[...].T.astype(BF16)


def _v_weight_transposed(w_in_l):
    d = w_in_l.shape[0]
    return pl.pallas_call(
        _transpose_cast_kernel,
        grid=(1,),
        in_specs=[pl.BlockSpec((d, d), lambda i: (0, C_VA // d))],
        out_specs=pl.BlockSpec((d, d), lambda i: (0, 0)),
        out_shape=jax.ShapeDtypeStruct((d, d), BF16),
        compiler_params=pltpu.CompilerParams(
            dimension_semantics=("arbitrary",),
            vmem_limit_bytes=V7X_VMEM_LIMIT),
        name="v_weight_t",
    )(w_in_l)


def _in_proj_kernel(x_ref, mod_ref, g1_ref, w_ref, wvt_ref, gq_ref, gk_ref,
                    cda_ref, sda_a_ref, sda_b_ref, cr_ref, sr_ref,
                    qa_ref, ka_ref, vt_ref, qkr_ref, vr_ref, ga_ref, gbs_ref):
    tm = x_ref.shape[0]
    x = x_ref[...]
    ms = jnp.mean(x * x, axis=-1, keepdims=True)
    h = x * lax.rsqrt(ms + EPS) * g1_ref[...]
    mod = mod_ref[0]
    hm = (h * (1.0 + mod[1:2, :]) + mod[0:1, :]).astype(BF16)

    def proj(col, width):
        return jnp.dot(hm, w_ref[:, col:col + width], preferred_element_type=F32)

    cda, sda_a, sda_b = cda_ref[...], sda_a_ref[...], sda_b_ref[...]
    first_map = lax.broadcasted_iota(jnp.int32, (tm, LANES), 1) < DA_QK_DIM

    def qk_norm_rope(acc, gain, out_ref):
        for j in range(D_MODEL // LANES):
            a = acc[:, j * LANES:(j + 1) * LANES]
            a2 = a * a
            ms1 = jnp.sum(jnp.where(first_map, a2, 0.0), axis=-1, keepdims=True)
            ms2 = jnp.sum(jnp.where(first_map, 0.0, a2), axis=-1, keepdims=True)
            rs = lax.rsqrt(jnp.where(first_map, ms1, ms2) * (1.0 / DA_QK_DIM) + EPS)
            a = a * rs * gain
            rot = (a * cda + pltpu.roll(a, LANES - ROPE_HALF, 1) * sda_a
                   + pltpu.roll(a, ROPE_HALF, 1) * sda_b)
            out_ref[:, j * LANES:(j + 1) * LANES] = rot.astype(BF16)

    qk_norm_rope(proj(C_QA, D_MODEL), gq_ref[...], qa_ref)
    qk_norm_rope(proj(C_KA, D_MODEL), gk_ref[...], ka_ref)

    vt = lax.dot_general(wvt_ref[...], hm, NT_DIMS, preferred_element_type=F32)
    vt_ref[0, 0] = vt.astype(BF16)

    qkr = proj(C_QR, D_MODEL)
    cr, sr = cr_ref[...], sr_ref[...]
    for j in range(D_MODEL // LANES):
        a = qkr[:, j * LANES:(j + 1) * LANES]
        rot = a * cr + pltpu.roll(a, LANES // 2, 1) * sr
        if j >= RET_HEADS:
            rot = rot * (RET_K_DIM ** -0.5)
        qkr_ref[:, j * LANES:(j + 1) * LANES] = rot.astype(BF16)

    vr_ref[...] = proj(C_VR, D_MODEL).astype(BF16)

    gr = proj(C_GR, D_MODEL)
    ga_ref[...] = jax.nn.sigmoid(proj(C_GA, D_MODEL)).astype(BF16)
    gbs_ref[...] = (jax.nn.sigmoid(proj(C_GB, D_MODEL))
                    * (gr * jax.nn.sigmoid(gr))).astype(BF16)


def _in_proj(x2, mod3, g1, w_in, w_vt, gq, gk, tabs, batch, seq):
    t, d = x2.shape
    tm = TM_PROJ
    tpb = seq // tm
    row = pl.BlockSpec((tm, d), lambda i: (i, 0))
    tab = pl.BlockSpec((tm, LANES), lambda i: (i % tpb, 0))
    vec = pl.BlockSpec((1, LANES), lambda i: (0, 0))
    out_row = jax.ShapeDtypeStruct((t, d), BF16)
    return pl.pallas_call(
        _in_proj_kernel,
        grid=(t // tm,),
        in_specs=[row,
                  pl.BlockSpec((1, N_MOD, d), lambda i: (i // tpb, 0, 0)),
                  pl.BlockSpec((1, d), lambda i: (0, 0)),
                  _resident((d, IN_COLS)),
                  _resident((d, d)),
                  vec, vec, tab, tab, tab, tab, tab],
        out_specs=[row, row,
                   pl.BlockSpec((1, 1, d, tm),
                                lambda i: (i // tpb, (i % tpb) // (TK // tm), 0,
                                           (i % tpb) % (TK // tm))),
                   row, row, row, row],
        out_shape=[out_row, out_row,
                   jax.ShapeDtypeStruct((batch, seq // TK, d, TK), BF16),
                   out_row, out_row, out_row, out_row],
        compiler_params=pltpu.CompilerParams(
            dimension_semantics=("arbitrary",),
            vmem_limit_bytes=V7X_VMEM_LIMIT),
        name="in_proj",
    )(x2, mod3, g1, w_in, w_vt, gq, gk, *tabs)


def _diff_attn_kernel(q_ref, k_ref, vt_ref, bias_ref, lam_ref, g_ref, o_ref,
                      qcat_ref, acc_ref, l_ref, m_ref, bound_ref, kmax_ref, early_ref, *, nq):
    qi = pl.program_id(2)
    heads = range(ATTN_HEADS_PER_STEP)

    def k_block(j, hd):
        koff = pl.multiple_of(j * TK, TK)
        return k_ref[0, pl.ds(koff, TK), hd * LANES:(hd + 1) * LANES]

    def v_block(j, hd):
        return vt_ref[0, j, hd * DA_V_DIM:(hd + 1) * DA_V_DIM, :]

    @pl.when(qi == 0)
    def _():
        gi = lax.broadcasted_iota(jnp.int32, (LANES, LANES), 0) // DA_QK_DIM
        gj = lax.broadcasted_iota(jnp.int32, (LANES, LANES), 1) // DA_QK_DIM
        group_sum = jnp.where(gi == gj, 1.0, 0.0).astype(BF16)
        for hd in heads:
            k = k_ref[0, :, hd * LANES:(hd + 1) * LANES].astype(F32)
            norm2 = jnp.dot((k * k).astype(BF16), group_sum, preferred_element_type=F32)
            kmax_ref[hd] = jnp.max(norm2, axis=0, keepdims=True)
        acc_ref[1] = jnp.zeros(acc_ref.shape[1:], F32)
        l_ref[1] = jnp.ones(l_ref.shape[1:], F32)

    def set_bounds():
        lane = lax.broadcasted_iota(jnp.int32, (TQ, LANES), 1)
        for hd in heads:
            q = q_ref[0, :, hd * LANES:(hd + 1) * LANES]
            zero = jnp.zeros_like(q)
            qcat = jnp.concatenate([jnp.where(lane < DA_QK_DIM, q, zero),
                                    jnp.where(lane >= DA_QK_DIM, q, zero)], axis=0)
            qcat_ref[hd] = qcat
            qsq = jnp.square(qcat.astype(F32)).astype(BF16)
            kmax = jnp.broadcast_to(kmax_ref[hd], (16, LANES)).astype(BF16)
            ub2 = lax.dot_general(kmax, qsq, NT_DIMS, preferred_element_type=F32)
            bound_ref[hd] = jnp.sqrt(ub2[0:1, :] * BOUND_MARGIN)

    def stream(blocks, with_diagonal, into):
        for hd in heads:
            pv, row_sum = None, None
            for n, j in enumerate(blocks):
                s = lax.dot_general(k_block(j, hd), qcat_ref[hd], NT_DIMS,
                                    preferred_element_type=F32)
                if with_diagonal and n == len(blocks) - 1:
                    s = s + bias_ref[...]
                e = jnp.exp2(s - bound_ref[hd])
                part_sum = jnp.sum(e, axis=0, keepdims=True)
                part = jnp.dot(v_block(j, hd), e.astype(BF16),
                               preferred_element_type=F32)
                pv = part if pv is None else pv + part
                row_sum = part_sum if row_sum is None else row_sum + part_sum
            acc_ref[into, hd] = pv if with_diagonal else acc_ref[into, hd] + pv
            l_ref[into, hd] = row_sum if with_diagonal else l_ref[into, hd] + row_sum
            if with_diagonal:
                first_min = jnp.min(row_sum)
                early_ref[0] = first_min if hd == 0 else jnp.minimum(early_ref[0], first_min)

    def write_out(src):
        for hd in heads:
            inv_l = 1.0 / l_ref[src, hd]
            o = (acc_ref[src, hd, :, :TQ] * inv_l[:, :TQ]
                 - acc_ref[src, hd, :, TQ:] * (lam_ref[...] * inv_l[:, TQ:]))
            ms = jnp.mean(o * o, axis=0, keepdims=True)
            on = o * lax.rsqrt(ms + EPS) * g_ref[...]
            o_ref[0, :, hd * LANES:(hd + 1) * LANES] = on.T.astype(BF16)

    def rest(into):
        def stream_body(t, carry):
            stream([2 * t, 2 * t + 1], False, into)
            return carry
        lax.fori_loop(0, qi // 2, stream_body, 0)

    working = qi < nq
    slot = qi % 2

    @pl.when(jnp.logical_and(slot == 0, working))
    def _():
        set_bounds()
        stream([qi], True, 0)
        write_out(1)
        rest(0)

    @pl.when(jnp.logical_and(slot == 1, working))
    def _():
        set_bounds()
        stream([qi - 1, qi], True, 1)
        write_out(0)
        rest(1)

    @pl.when(qi == nq)
    def _():
        write_out((nq - 1) % 2)

    def online_step(j, carry):
        for hd in heads:
            s = lax.dot_general(k_block(j, hd), qcat_ref[hd], NT_DIMS,
                                preferred_element_type=F32)
            s = s + jnp.where(j == qi, bias_ref[...], 0.0)
            m_old = m_ref[hd]
            m_new = jnp.maximum(m_old, jnp.max(s, axis=0, keepdims=True))
            alpha = jnp.exp2(m_old - m_new)
            e = jnp.exp2(s - m_new)
            pv = jnp.dot(v_block(j, hd), e.astype(BF16), preferred_element_type=F32)
            acc_ref[slot, hd] = alpha * acc_ref[slot, hd] + pv
            l_ref[slot, hd] = alpha * l_ref[slot, hd] + jnp.sum(e, axis=0, keepdims=True)
            m_ref[hd] = m_new
        return carry

    @pl.when(jnp.logical_and(working, jnp.logical_not(early_ref[0] >= ROW_SUM_FLOOR)))
    def _():
        @pl.when(jnp.logical_not(jnp.min(l_ref[slot]) >= ROW_SUM_FLOOR))
        def _():
            m_ref[...] = jnp.full(m_ref.shape, -jnp.inf, F32)
            acc_ref[slot] = jnp.zeros(acc_ref.shape[1:], F32)
            l_ref[slot] = jnp.zeros(l_ref.shape[1:], F32)
            lax.fori_loop(0, qi + 1, online_step, 0)


def _diag_bias():
    kc = np.arange(TK)[:, None] // CHUNK
    qc = (np.arange(2 * TQ)[None, :] % TQ) // CHUNK
    return jnp.asarray(np.where(kc <= qc, 0.0, -np.inf).astype(np.float32))


def _diff_attn(qa, ka, vt, lam, g_col, batch, seq):
    hps = ATTN_HEADS_PER_STEP
    nq = seq // TQ
    return pl.pallas_call(
        functools.partial(_diff_attn_kernel, nq=nq),
        grid=(batch, DA_HEADS // hps, nq + 1),
        in_specs=[pl.BlockSpec((1, TQ, hps * LANES),
                               lambda b, h, i: (b, jnp.minimum(i, nq - 1), h)),
                  pl.BlockSpec((1, seq, hps * LANES), lambda b, h, i: (b, 0, h)),
                  pl.BlockSpec((1, seq // TK, hps * DA_V_DIM, TK), lambda b, h, i: (b, 0, h, 0)),
                  _resident((TK, 2 * TQ)),
                  pl.BlockSpec((1, 1), lambda b, h, i: (0, 0)),
                  pl.BlockSpec((DA_V_DIM, 1), lambda b, h, i: (0, 0))],
        out_specs=pl.BlockSpec((1, TQ, hps * LANES),
                               lambda b, h, i: (b, jnp.maximum(i - 1, 0), h)),
        out_shape=jax.ShapeDtypeStruct((batch, seq, D_MODEL), BF16),
        scratch_shapes=[pltpu.VMEM((hps, 2 * TQ, LANES), BF16),
                        pltpu.VMEM((2, hps, DA_V_DIM, 2 * TQ), F32),
                        pltpu.VMEM((2, hps, 1, 2 * TQ), F32),
                        pltpu.VMEM((hps, 1, 2 * TQ), F32),
                        pltpu.VMEM((hps, 1, 2 * TQ), F32),
                        pltpu.VMEM((hps, 1, LANES), F32),
                        pltpu.SMEM((1,), F32)],
        compiler_params=pltpu.CompilerParams(
            dimension_semantics=("arbitrary", "arbitrary", "arbitrary"),
            vmem_limit_bytes=V7X_VMEM_LIMIT),
        name="diff_attn",
    )(qa, ka, vt, _diag_bias(), lam, g_col)


def _retention_kernel(qk_ref, v_ref, g_ref, intra_ref, inner_ref, kvd_ref, o_ref,
                      state_ref, *, chunk_decay):
    @pl.when(pl.program_id(1) == 0)
    def _():
        state_ref[...] = jnp.zeros(state_ref.shape, F32)

    cr = intra_ref.shape[1]
    for ci in range(RET_CHUNKS_PER_STEP):
        rows = slice(ci * cr, (ci + 1) * cr)
        for hd in range(RET_HEADS):
            q = qk_ref[0, rows, hd * RET_K_DIM:(hd + 1) * RET_K_DIM]
            k = qk_ref[0, rows, (RET_HEADS + hd) * RET_K_DIM:(RET_HEADS + hd + 1) * RET_K_DIM]
            v = v_ref[0, rows, hd * RET_V_DIM:(hd + 1) * RET_V_DIM]
            sc = lax.dot_general(q, k, NT_DIMS, preferred_element_type=F32) * intra_ref[hd]
            y = jnp.dot(sc.astype(BF16), v, preferred_element_type=F32)
            st = state_ref[hd]
            cross = jnp.dot(q, st.astype(BF16), preferred_element_type=F32)
            y = y + cross * inner_ref[hd]
            kd = (k.astype(F32) * kvd_ref[hd]).astype(BF16)
            upd = lax.dot_general(kd, v, TN_DIMS, preferred_element_type=F32)
            state_ref[hd] = st * chunk_decay[hd] + upd
            ms = jnp.mean(y * y, axis=-1, keepdims=True)
            o_ref[0, rows, hd * RET_V_DIM:(hd + 1) * RET_V_DIM] = (
                y * lax.rsqrt(ms + EPS) * g_ref[...]).astype(BF16)


def _retention_tables(log_gamma, cr):
    n = np.arange(cr, dtype=np.float32)
    lg = log_gamma.astype(np.float32)[:, None, None]
    diff = n[:, None] - n[None, :]
    intra = np.where(diff >= 0, np.exp(np.maximum(diff, 0.0) * lg), 0.0).astype(np.float32)
    inner = np.exp((n + 1.0)[None, :, None] * lg).astype(np.float32)
    kvd = np.exp((cr - 1.0 - n)[None, :, None] * lg).astype(np.float32)
    inner = np.broadcast_to(inner, (RET_HEADS, cr, RET_V_DIM))
    kvd = np.broadcast_to(kvd, (RET_HEADS, cr, RET_K_DIM))
    chunk = tuple(float(np.exp(np.float32(cr) * g)) for g in log_gamma.astype(np.float32))
    return jnp.asarray(intra), jnp.asarray(inner), jnp.asarray(kvd), chunk


def _retention(qkr, vr, g_ret, log_gamma, batch, seq):
    cr = RET_CHUNK
    intra, inner, kvd, chunk_decay = _retention_tables(log_gamma, cr)
    step_rows = cr * RET_CHUNKS_PER_STEP
    rows = pl.BlockSpec((1, step_rows, D_MODEL), lambda b, c: (b, c, 0))
    return pl.pallas_call(
        functools.partial(_retention_kernel, chunk_decay=chunk_decay),
        grid=(batch, seq // step_rows),
        in_specs=[rows, rows,
                  pl.BlockSpec((1, RET_V_DIM), lambda b, c: (0, 0)),
                  _resident((RET_HEADS, cr, cr)),
                  _resident((RET_HEADS, cr, RET_V_DIM)),
                  _resident((RET_HEADS, cr, RET_K_DIM))],
        out_specs=rows,
        out_shape=jax.ShapeDtypeStruct((batch, seq, D_MODEL), BF16),
        scratch_shapes=[pltpu.VMEM((RET_HEADS, RET_K_DIM, RET_V_DIM), F32)],
        compiler_params=pltpu.CompilerParams(
            dimension_semantics=("arbitrary", "arbitrary"),
            vmem_limit_bytes=V7X_VMEM_LIMIT),
        name="retention",
    )(qkr, vr, g_ret, intra, inner, kvd)


def _out_mlp_kernel(x_ref, oa_ref, or_ref, ga_ref, gbs_ref, mod_ref, g2_ref,
                    wo_ref, wu_ref, wd_ref, out_ref):
    mod = mod_ref[0]
    u = (ga_ref[...].astype(F32) * oa_ref[...].astype(F32)
         + gbs_ref[...].astype(F32) * or_ref[...].astype(F32)).astype(BF16)
    y = jnp.dot(u, wo_ref[...], preferred_element_type=F32)
    x1 = x_ref[...] + mod[2:3, :] * y
    ms = jnp.mean(x1 * x1, axis=-1, keepdims=True)
    hn = x1 * lax.rsqrt(ms + EPS) * g2_ref[...]
    hff = (hn * (1.0 + mod[4:5, :]) + mod[3:4, :]).astype(BF16)
    up = jnp.dot(hff, wu_ref[...], preferred_element_type=F32)
    act = jnp.square(jnp.maximum(up, 0.0)).astype(BF16)
    ff = jnp.dot(act, wd_ref[...], preferred_element_type=F32)
    out_ref[...] = x1 + mod[5:6, :] * ff


def _out_mlp(x2, oa, orr, ga, gbs, mod3, g2, w_out, w_up, w_down, seq):
    t, d = x2.shape
    tm = TM_MLP
    tpb = seq // tm
    row = pl.BlockSpec((tm, d), lambda i: (i, 0))
    return pl.pallas_call(
        _out_mlp_kernel,
        grid=(t // tm,),
        in_specs=[row, row, row, row, row,
                  pl.BlockSpec((1, N_MOD, d), lambda i: (i // tpb, 0, 0)),
                  pl.BlockSpec((1, d), lambda i: (0, 0)),
                  _resident((d, d)), _resident((d, D_FF)), _resident((D_FF, d))],
        out_specs=row,
        out_shape=jax.ShapeDtypeStruct((t, d), F32),
        compiler_params=pltpu.CompilerParams(
            dimension_semantics=("arbitrary",),
            vmem_limit_bytes=V7X_VMEM_LIMIT),
        name="out_mlp",
    )(x2, oa, orr, ga, gbs, mod3, g2, w_out, w_up, w_down)


def _rope_tables(seq):
    f32 = np.float32
    pos = np.arange(seq, dtype=f32)[:, None]
    da_inv = (f32(ROPE_THETA) ** (-np.arange(0, DA_QK_DIM, 2, dtype=f32) / f32(DA_QK_DIM))).astype(f32)
    ang = (pos * da_inv[None, :]).astype(f32)
    cos, sin = np.cos(ang).astype(f32), np.sin(ang).astype(f32)
    zero = np.zeros_like(sin)
    cda = np.tile(cos, (1, 4))
    sda_a = np.tile(np.concatenate([-sin, zero], axis=1), (1, 2))
    sda_b = np.tile(np.concatenate([zero, sin], axis=1), (1, 2))
    ret_inv = (f32(1.0) / (f32(ROPE_THETA) ** np.linspace(0.0, 1.0, RET_K_DIM // 2, dtype=f32))).astype(f32)
    ang_r = (pos * ret_inv[None, :]).astype(f32)
    cos_r, sin_r = np.cos(ang_r).astype(f32), np.sin(ang_r).astype(f32)
    cr = np.concatenate([cos_r, cos_r], axis=1)
    sr = np.concatenate([-sin_r, sin_r], axis=1)
    return tuple(jnp.asarray(t) for t in (cda, sda_a, sda_b, cr, sr))


def kernel(x, c, w_ada, b_ada, g_norm1, w_in, g_q, g_k, lambda_q1, lambda_k1,
           lambda_q2, lambda_k2, g_da_out, g_ret_out, w_out, g_norm2, w_up, w_down):
    batch, seq, d = x.shape
    depth = w_ada.shape[0]
    assert d == D_MODEL and w_in.shape[1:] == (D_MODEL, IN_COLS), (x.shape, w_in.shape)
    for tile in (TM_PROJ, TM_MLP, TQ, TK, RET_CHUNK * RET_CHUNKS_PER_STEP):
        assert seq % tile == 0, (seq, tile)
    assert TQ == TK and TK % TM_PROJ == 0 and TQ % CHUNK == 0
    log_gamma = np.log(1.0 - 2.0 ** (-5.0 - np.arange(RET_HEADS))).astype(np.float32)
    tabs = _rope_tables(seq)

    for l in range(depth):
        lambda_init = 0.8 - 0.6 * math.exp(-0.3 * l)
        mod, lam = _modulation(c, w_ada[l], b_ada[l][None, :],
                               lambda_q1[l][None, :], lambda_k1[l][None, :],
                               lambda_q2[l][None, :], lambda_k2[l][None, :],
                               lambda_init)
        mod3 = mod.reshape(batch, N_MOD, d)
        x2 = x.reshape(batch * seq, d)

        w_in_b = w_in[l].astype(BF16)
        w_vt = _v_weight_transposed(w_in[l])
        gq = (jnp.tile(g_q[l], 2) * (DA_QK_DIM ** -0.5 * LOG2E))[None, :]
        gk = jnp.tile(g_k[l], 2)[None, :]
        qa, ka, vt, qkr, vr, ga, gbs = _in_proj(
            x2, mod3, g_norm1[l][None, :], w_in_b, w_vt, gq, gk, tabs, batch, seq)

        g_col = (g_da_out[l] * (1.0 - lambda_init))[:, None]
        oa = _diff_attn(qa.reshape(batch, seq, d), ka.reshape(batch, seq, d), vt,
                        lam, g_col, batch, seq)
        orr = _retention(qkr.reshape(batch, seq, d), vr.reshape(batch, seq, d),
                         g_ret_out[l][None, :], log_gamma, batch, seq)

        out = _out_mlp(x2, oa.reshape(batch * seq, d), orr.reshape(batch * seq, d),
                       ga, gbs, mod3, g_norm2[l][None, :],
                       w_out[l].astype(BF16), w_up[l].astype(BF16),
                       w_down[l].astype(BF16), seq)
        x = out.reshape(batch, seq, d)
    return x
```
